```python
import math
import jax, jax.numpy as jnp
from jax import lax
import numpy as np

D_MODEL = 4096
BATCH = 4
SEQ = 2048
DEPTH = 2
DEC_BATCH = 128
DEC_SEQ = 4
PAST_LEN = 16384
PAGE_SIZE = 128

N_EVEN = (DEPTH + 1) // 2
N_ODD = DEPTH // 2
CHUNK = 128
A_GROUPS = 16
A_GROUP_DIM = 128
W_A = A_GROUPS * A_GROUP_DIM
MLA_HEADS = 16
QK_NOPE = 128
QK_ROPE = 64
V_DIM = 128
Q_RANK = 1024
KV_RANK = 512
W_B = MLA_HEADS * V_DIM
ROPE_THETA = 10000.0
QBLOCK = 128
ATTN_SCALE = 1.0 / math.sqrt(QK_NOPE + QK_ROPE)
W_C = D_MODEL
CONV_W = 31
EPS = 1e-6
EVEN_SPLIT = (W_A, W_A, W_A, Q_RANK, KV_RANK, QK_ROPE, W_B)
EVEN_IN = sum(EVEN_SPLIT)

kernel_name = 'hybrid_gmlp_mla_conformer_step'


def _split_points(sizes):
    return [int(s) for s in np.cumsum(sizes)[:-1]]


def rms_norm(x, g):
    xf = x.astype(jnp.float32)
    y = xf * lax.rsqrt(jnp.mean(xf * xf, axis=-1, keepdims=True) + EPS)
    return (y * g.astype(jnp.float32)).astype(x.dtype)


def layer_norm(x, g, b):
    xf = x.astype(jnp.float32)
    mu = jnp.mean(xf, axis=-1, keepdims=True)
    xc = xf - mu
    y = xc * lax.rsqrt(jnp.mean(xc * xc, axis=-1, keepdims=True) + EPS)
    return (y * g.astype(jnp.float32) + b.astype(jnp.float32)).astype(x.dtype)


def rope_tables(length, offset):
    inv = jnp.power(ROPE_THETA, -jnp.arange(0, QK_ROPE, 2, dtype=jnp.float32) / QK_ROPE)
    pos = jnp.arange(length, dtype=jnp.float32) + offset
    ang = pos[:, None] * inv[None, :]
    return jnp.cos(ang), jnp.sin(ang)


def apply_rope(x, cos, sin):
    xf = x.astype(jnp.float32)
    half = QK_ROPE // 2
    x1, x2 = xf[..., :half], xf[..., half:]
    return jnp.concatenate([x1 * cos - x2 * sin, x2 * cos + x1 * sin], axis=-1).astype(x.dtype)


def chunk_spatial_mix(v, w_s, b_s):
    B, L, C = v.shape
    cl = min(CHUNK, L)
    n = -(-L // cl)
    vp = jnp.pad(v, ((0, 0), (0, n * cl - L), (0, 0))).reshape(B, n, cl, A_GROUPS, A_GROUP_DIM)
    wm = (w_s * jnp.tril(jnp.ones((CHUNK, CHUNK), w_s.dtype)))[:, :cl, :cl]
    out = jnp.einsum('gts,bnsgd->bntgd', wm, vp) + b_s[:, :cl].T[:, :, None]
    return out.reshape(B, n * cl, C)[:, :L]


def mla_prefill(q_abs, q_pe, c, k_pe):
    B, L, H, R = q_abs.shape
    nb = -(-L // QBLOCK)
    pad = nb * QBLOCK - L

    def blocks(t):
        t = jnp.pad(t, ((0, 0), (0, pad), (0, 0), (0, 0)))
        return t.reshape(B, nb, QBLOCK, H, t.shape[-1]).swapaxes(0, 1)

    k_pos = jnp.arange(L)
    c32 = c.astype(jnp.float32)

    def one_block(args):
        qa, qp, start = args
        s = (jnp.einsum('bqhr,bkr->bhqk', qa, c, preferred_element_type=jnp.float32)
             + jnp.einsum('bqhp,bkp->bhqk', qp, k_pe, preferred_element_type=jnp.float32)) * ATTN_SCALE
        q_pos = start + jnp.arange(QBLOCK)
        s = jnp.where(k_pos[None, :] <= q_pos[:, None], s, -jnp.inf)
        p = jax.nn.softmax(s, axis=-1)
        return jnp.einsum('bhqk,bkr->bqhr', p, c32)

    o = lax.map(one_block, (blocks(q_abs), blocks(q_pe), jnp.arange(nb) * QBLOCK))
    return o.swapaxes(0, 1).reshape(B, nb * QBLOCK, H, R)[:, :L].astype(q_abs.dtype)


def mla_decode(q_abs, q_pe, c_new, kpe_new, cache_c, cache_kpe, layer, page_table):
    Bd, Lq, H, R = q_abs.shape
    qa = q_abs.astype(jnp.float32)
    qp = q_pe.astype(jnp.float32)

    def scores(c, kpe):
        return (jnp.einsum('bqhr,bkr->bhqk', qa, c.astype(jnp.float32))
                + jnp.einsum('bqhp,bkp->bhqk', qp, kpe.astype(jnp.float32))) * ATTN_SCALE

    def update(carry, s, c):
        m, l, acc = carry
        m_new = jnp.maximum(m, jnp.max(s, axis=-1))
        alpha = jnp.exp(m - m_new)
        p = jnp.exp(s - m_new[..., None])
        return (m_new, l * alpha + jnp.sum(p, axis=-1),
                acc * alpha[..., None] + jnp.einsum('bhqk,bkr->bhqr', p, c.astype(jnp.float32)))

    def step(carry, pages):
        c_blk = cache_c[layer, pages]
        k_blk = cache_kpe[layer, pages]
        return update(carry, scores(c_blk, k_blk), c_blk), None

    init = (jnp.full((Bd, H, Lq), -jnp.inf, jnp.float32),
            jnp.zeros((Bd, H, Lq), jnp.float32),
            jnp.zeros((Bd, H, Lq, R), jnp.float32))
    carry, _ = lax.scan(step, init, page_table.T)
    s_new = jnp.where(jnp.tril(jnp.ones((Lq, Lq), bool)), scores(c_new, kpe_new), -jnp.inf)
    m, l, acc = update(carry, s_new, c_new)
    o = acc / l[..., None]
    return o.transpose(0, 2, 1, 3).astype(q_abs.dtype)


def even_mixer(h, offset, w_in, ln_v_g, ln_v_b, w_s, b_s, g_q_a, w_q_b, g_kv_a, w_kv_b, w_out, past):
    B, L, _ = h.shape
    u, v, z_a, q_a, kv_a, k_r, z_b = jnp.split(h @ w_in, _split_points(EVEN_SPLIT), axis=-1)
    v_n = layer_norm(jax.nn.gelu(v), ln_v_g, ln_v_b)
    out_a = jax.nn.gelu(u) * chunk_spatial_mix(v_n, w_s, b_s) * jax.nn.silu(z_a)
    q = (rms_norm(q_a, g_q_a) @ w_q_b).reshape(B, L, MLA_HEADS, QK_NOPE + QK_ROPE)
    cos, sin = rope_tables(L, offset)
    q_pe = apply_rope(q[..., QK_NOPE:], cos[:, None, :], sin[:, None, :])
    k_pe = apply_rope(k_r, cos, sin)
    c = rms_norm(kv_a, g_kv_a)
    w_kvb = w_kv_b.reshape(KV_RANK, MLA_HEADS, QK_NOPE + V_DIM)
    q_abs = jnp.einsum('blhd,rhd->blhr', q[..., :QK_NOPE], w_kvb[..., :QK_NOPE])
    if past is None:
        o_lat = mla_prefill(q_abs, q_pe, c, k_pe)
    else:
        o_lat = mla_decode(q_abs, q_pe, c, k_pe, *past)
    out_b = jnp.einsum('blhr,rhd->blhd', o_lat, w_kvb[..., QK_NOPE:]).reshape(B, L, W_B) * jax.nn.silu(z_b)
    y = jnp.concatenate([out_a, out_b], axis=-1) @ w_out
    v_open = v_n[:, ((L - 1) // CHUNK) * CHUNK:]
    return y, c, k_pe, v_open


def odd_mixer(h, conv_prev, w_in, w_dw, b_dw, ln_c_g, ln_c_b, w_out):
    B, L, _ = h.shape
    a, b, z = jnp.split(h @ w_in, 3, axis=-1)
    g = a * jax.nn.sigmoid(b)
    if conv_prev is None:
        conv_prev = jnp.zeros((B, CONV_W - 1, W_C), g.dtype)
    xp = jnp.concatenate([conv_prev.astype(g.dtype), g], axis=1)
    d = lax.conv_general_dilated(xp, w_dw[:, None, :].astype(xp.dtype), window_strides=(1,), padding='VALID',
                                 dimension_numbers=('NWC', 'WIO', 'NWC'), feature_group_count=W_C) + b_dw
    y = jax.nn.silu(layer_norm(d, ln_c_g, ln_c_b)) * jax.nn.silu(z)
    return y @ w_out, xp[:, -(CONV_W - 1):]


def setup_inputs(seed: int = 0) -> dict:
    key = jax.random.key(seed)
    ks = jax.random.split(key, 24)

    def nrm(k, shape, scale):
        return jax.random.normal(k, shape, jnp.float32) * scale

    n_pages = PAST_LEN // PAGE_SIZE
    n_used = DEC_BATCH * n_pages
    n_phys = n_used + max(1, n_used // 4)
    page_table = jax.random.permutation(ks[0], n_phys)[:n_used].reshape(DEC_BATCH, n_pages).astype(jnp.int32)
    return {
        'x_prompt': nrm(ks[1], (BATCH, SEQ, D_MODEL), 1.0),
        'x_sample': nrm(ks[2], (DEC_BATCH, DEC_SEQ, D_MODEL), 1.0),
        'cache_kv_latent': nrm(ks[3], (N_EVEN, n_phys, PAGE_SIZE, KV_RANK), 1.0),
        'cache_k_rope': nrm(ks[4], (N_EVEN, n_phys, PAGE_SIZE, QK_ROPE), 1.0),
        'state_conv': nrm(ks[5], (N_ODD, DEC_BATCH, CONV_W - 1, W_C), 0.5),
        'page_table': page_table,
        'w_in_even': nrm(ks[6], (N_EVEN, D_MODEL, EVEN_IN), D_MODEL ** -0.5),
        'ln_v_g': 1.0 + nrm(ks[7], (N_EVEN, W_A), 0.02),
        'ln_v_b': nrm(ks[8], (N_EVEN, W_A), 0.02),
        'w_s': nrm(ks[9], (N_EVEN, A_GROUPS, CHUNK, CHUNK), CHUNK ** -0.5),
        'b_s': 1.0 + nrm(ks[10], (N_EVEN, A_GROUPS, CHUNK), 0.02),
        'g_q_a': 1.0 + nrm(ks[11], (N_EVEN, Q_RANK), 0.02),
        'w_q_b': nrm(ks[12], (N_EVEN, Q_RANK, MLA_HEADS * (QK_NOPE + QK_ROPE)), Q_RANK ** -0.5),
        'g_kv_a': 1.0 + nrm(ks[13], (N_EVEN, KV_RANK), 0.02),
        'w_kv_b': nrm(ks[14], (N_EVEN, KV_RANK, MLA_HEADS * (QK_NOPE + V_DIM)), KV_RANK ** -0.5),
        'w_out_even': nrm(ks[15], (N_EVEN, W_A + W_B, D_MODEL), (W_A + W_B) ** -0.5),
        'w_in_odd': nrm(ks[16], (N_ODD, D_MODEL, 3 * W_C), D_MODEL ** -0.5),
        'w_dw': nrm(ks[17], (N_ODD, CONV_W, W_C), CONV_W ** -0.5),
        'b_dw': nrm(ks[18], (N_ODD, W_C), 0.02),
        'ln_c_g': 1.0 + nrm(ks[19], (N_ODD, W_C), 0.02),
        'ln_c_b': nrm(ks[20], (N_ODD, W_C), 0.02),
        'w_out_odd': nrm(ks[21], (N_ODD, W_C, D_MODEL), W_C ** -0.5),
        'g_pre': 1.0 + nrm(ks[22], (DEPTH, D_MODEL), 0.02),
        'g_post': 1.0 + nrm(ks[23], (DEPTH, D_MODEL), 0.02),
    }


def reference(x_prompt, x_sample, cache_kv_latent, cache_k_rope, state_conv, page_table,
              w_in_even, ln_v_g, ln_v_b, w_s, b_s, g_q_a, w_q_b, g_kv_a, w_kv_b, w_out_even,
              w_in_odd, w_dw, b_dw, ln_c_g, ln_c_b, w_out_odd, g_pre, g_post):
    xp, xs = x_prompt, x_sample
    lat_p, kpe_p, v_p, conv_p = [], [], [], []
    lat_s, kpe_s, v_s, conv_s = [], [], [], []
    for i in range(DEPTH):
        hp = rms_norm(xp, g_pre[i])
        hs = rms_norm(xs, g_pre[i])
        j = i // 2
        if i % 2 == 0:
            wts = (w_in_even[j], ln_v_g[j], ln_v_b[j], w_s[j], b_s[j], g_q_a[j], w_q_b[j],
                   g_kv_a[j], w_kv_b[j], w_out_even[j])
            yp, c, k, v = even_mixer(hp, 0, *wts, None)
            lat_p.append(c)
            kpe_p.append(k)
            v_p.append(v)
            ys, c, k, v = even_mixer(hs, PAST_LEN, *wts, (cache_kv_latent, cache_k_rope, j, page_table))
            lat_s.append(c)
            kpe_s.append(k)
            v_s.append(v)
        else:
            wts = (w_in_odd[j], w_dw[j], b_dw[j], ln_c_g[j], ln_c_b[j], w_out_odd[j])
            yp, cv = odd_mixer(hp, None, *wts)
            conv_p.append(cv)
            ys, cv = odd_mixer(hs, state_conv[j], *wts)
            conv_s.append(cv)
        xp = xp + rms_norm(yp, g_post[i])
        xs = xs + rms_norm(ys, g_post[i])
    return (xp, xs, jnp.stack(lat_p), jnp.stack(kpe_p), jnp.stack(v_p), jnp.stack(conv_p),
            jnp.stack(lat_s), jnp.stack(kpe_s), jnp.stack(v_s), jnp.stack(conv_s))
```

```python
import functools
import math

import jax
import jax.numpy as jnp
from jax import lax
from jax.experimental import pallas as pl
from jax.experimental.pallas import tpu as pltpu

F32 = jnp.float32
BF16 = jnp.bfloat16

EPS = 1e-6
LANES = 128
CHUNK = 128
A_GROUPS = 16
A_GROUP_DIM = 128
W_A = A_GROUPS * A_GROUP_DIM
HEADS = 16
QK_NOPE = 128
QK_ROPE = 64
V_DIM = 128
Q_RANK = 1024
KV_RANK = 512
W_B = HEADS * V_DIM
K_CAT = KV_RANK + QK_ROPE
ROPE_THETA = 10000.0
QBLOCK = 128
PAGE = 128
CONV_W = 31
CONV_HALO = 32
ATTN_SCALE = 1.0 / math.sqrt(QK_NOPE + QK_ROPE)
NEG_INF = float("-inf")
VMEM_LIMIT_BYTES = 56 * 1024 * 1024


def _params(n_axes):
    return pltpu.CompilerParams(dimension_semantics=("parallel",) * n_axes,
                                vmem_limit_bytes=VMEM_LIMIT_BYTES)


def _params_seq(sem):
    return pltpu.CompilerParams(dimension_semantics=sem, vmem_limit_bytes=VMEM_LIMIT_BYTES)


def _sigmoid(x):
    return 1.0 / (1.0 + jnp.exp(-x))


def _silu(x):
    return x * _sigmoid(x)


def _rms(x, g):
    return x * lax.rsqrt(jnp.mean(x * x, axis=-1, keepdims=True) + EPS) * g


def _layer_norm(x, g, b):
    mu = jnp.mean(x, axis=-1, keepdims=True)
    xc = x - mu
    return xc * lax.rsqrt(jnp.mean(xc * xc, axis=-1, keepdims=True) + EPS) * g + b


def _tile(n, pref):
    return pref if n % pref == 0 else n


def _rmsnorm_kernel(x_ref, g_ref, o_ref):
    o_ref[...] = _rms(x_ref[...], g_ref[...]).astype(o_ref.dtype)


def rmsnorm_cast(x, g):
    m, d = x.shape
    tm = _tile(m, 256)
    return pl.pallas_call(
        _rmsnorm_kernel,
        grid=(m // tm,),
        in_specs=[pl.BlockSpec((tm, d), lambda i: (i, 0)), pl.BlockSpec((1, d), lambda i: (0, 0))],
        out_specs=pl.BlockSpec((tm, d), lambda i: (i, 0)),
        out_shape=jax.ShapeDtypeStruct((m, d), BF16),
        compiler_params=_params(1),
        name="rmsnorm_cast",
    )(x, g.reshape(1, d))


def _residual_norm_kernel(x_ref, y_ref, g_ref, o_ref):
    o_ref[...] = x_ref[...] + _rms(y_ref[...], g_ref[...])


def residual_norm(x, y, g):
    m, d = x.shape
    tm = _tile(m, 256)
    return pl.pallas_call(
        _residual_norm_kernel,
        grid=(m // tm,),
        in_specs=[pl.BlockSpec((tm, d), lambda i: (i, 0)), pl.BlockSpec((tm, d), lambda i: (i, 0)),
                  pl.BlockSpec((1, d), lambda i: (0, 0))],
        out_specs=pl.BlockSpec((tm, d), lambda i: (i, 0)),
        out_shape=jax.ShapeDtypeStruct((m, d), F32),
        compiler_params=_params(1),
        name="residual_norm",
    )(x, y, g.reshape(1, d))


def _mm_act_kernel(x_ref, w_ref, o_ref, *, n_gelu_tiles):
    acc = jnp.dot(x_ref[...], w_ref[...], preferred_element_type=F32)
    j = pl.program_id(1)

    @pl.when(j < n_gelu_tiles)
    def _():
        o_ref[...] = jax.nn.gelu(acc).astype(o_ref.dtype)

    @pl.when(j >= n_gelu_tiles)
    def _():
        o_ref[...] = _silu(acc).astype(o_ref.dtype)


def mm_act(x, w, *, col0, ncols, n_gelu_cols, out_dtype):
    m, k = x.shape
    tm = _tile(m, 1024)
    tn = _tile(ncols, 1024)
    assert col0 % tn == 0 and n_gelu_cols % tn == 0
    j0 = col0 // tn
    return pl.pallas_call(
        functools.partial(_mm_act_kernel, n_gelu_tiles=n_gelu_cols // tn),
        grid=(m // tm, ncols // tn),
        in_specs=[pl.BlockSpec((tm, k), lambda i, j: (i, 0)),
                  pl.BlockSpec((k, tn), lambda i, j: (0, j + j0))],
        out_specs=pl.BlockSpec((tm, tn), lambda i, j: (i, j)),
        out_shape=jax.ShapeDtypeStruct((m, ncols), out_dtype),
        compiler_params=_params(2),
        name="mm_act",
    )(x, w)


def _mm_plain_kernel(*refs, n_in):
    x_refs = refs[:n_in]
    w_refs = refs[n_in:2 * n_in]
    o_ref = refs[2 * n_in]
    acc = jnp.dot(x_refs[0][...], w_refs[0][...], preferred_element_type=F32)
    for x_ref, w_ref in zip(x_refs[1:], w_refs[1:]):
        acc += jnp.dot(x_ref[...], w_ref[...], preferred_element_type=F32)
    o_ref[...] = acc


def mm_plain(xs, w):
    m = xs[0].shape[0]
    n = w.shape[1]
    tm = _tile(m, 1024)
    tn = _tile(n, 1024)
    in_specs = [pl.BlockSpec((tm, x.shape[1]), lambda i, j: (i, 0)) for x in xs]
    row = 0
    for x in xs:
        kx = x.shape[1]
        assert row % kx == 0
        in_specs.append(pl.BlockSpec((kx, tn), functools.partial(lambda i, j, r: (r, j), r=row // kx)))
        row += kx
    assert row == w.shape[0]
    return pl.pallas_call(
        functools.partial(_mm_plain_kernel, n_in=len(xs)),
        grid=(m // tm, n // tn),
        in_specs=in_specs,
        out_specs=pl.BlockSpec((tm, tn), lambda i, j: (i, j)),
        out_shape=jax.ShapeDtypeStruct((m, n), F32),
        compiler_params=_params(2),
        name="mm_plain",
    )(*xs, *([w] * len(xs)))


def _mm_rms_kernel(x_ref, w_ref, g_ref, o_ref):
    acc = jnp.dot(x_ref[...], w_ref[...], preferred_element_type=F32)
    o_ref[...] = _rms(acc, g_ref[...]).astype(o_ref.dtype)


def mm_rms(x, w, g):
    m, k = x.shape
    n = w.shape[1]
    tm = _tile(m, 1024)
    return pl.pallas_call(
        _mm_rms_kernel,
        grid=(m // tm,),
        in_specs=[pl.BlockSpec((tm, k), lambda i: (i, 0)), pl.BlockSpec((k, n), lambda i: (0, 0)),
                  pl.BlockSpec((1, n), lambda i: (0, 0))],
        out_specs=pl.BlockSpec((tm, n), lambda i: (i, 0)),
        out_shape=jax.ShapeDtypeStruct((m, n), BF16),
        compiler_params=_params(1),
        name="mm_rms",
    )(x, w, g.reshape(1, n))


def _rope_from_pair(pair, cs):
    t = pair * cs
    return (t + pltpu.roll(t, QK_ROPE, axis=1))[:, :QK_ROPE]


def _mm_kv_kernel(x_ref, w_ref, g_ref, cs_ref, c_ref, kpe_ref, kcat_ref):
    acc = jnp.dot(x_ref[...], w_ref[...], preferred_element_type=F32)
    c = _rms(acc[:, :KV_RANK], g_ref[...])
    kpe = _rope_from_pair(acc[:, KV_RANK:], cs_ref[...])
    c_ref[...] = c
    kpe_ref[...] = kpe
    kcat_ref[:, :KV_RANK] = c.astype(BF16)
    kcat_ref[:, KV_RANK:] = kpe.astype(BF16)


def mm_kv(x, w_kv, g_kv, cs):
    m, k = x.shape
    n = w_kv.shape[1]
    tm = _tile(m, 1024)
    return pl.pallas_call(
        _mm_kv_kernel,
        grid=(m // tm,),
        in_specs=[pl.BlockSpec((tm, k), lambda i: (i, 0)), pl.BlockSpec((k, n), lambda i: (0, 0)),
                  pl.BlockSpec((1, KV_RANK), lambda i: (0, 0)), pl.BlockSpec((tm, LANES), lambda i: (i, 0))],
        out_specs=[pl.BlockSpec((tm, KV_RANK), lambda i: (i, 0)), pl.BlockSpec((tm, QK_ROPE), lambda i: (i, 0)),
                   pl.BlockSpec((tm, K_CAT), lambda i: (i, 0))],
        out_shape=[jax.ShapeDtypeStruct((m, KV_RANK), F32), jax.ShapeDtypeStruct((m, QK_ROPE), F32),
                   jax.ShapeDtypeStruct((m, K_CAT), BF16)],
        compiler_params=_params(1),
        name="mm_kv",
    )(x, w_kv, g_kv.reshape(1, KV_RANK), cs)


def _mm_q_kernel(x_ref, wq_ref, wn_ref, cs_ref, o_ref):
    x = x_ref[...]
    cs = cs_ref[...]
    groups = x.shape[0] // QBLOCK
    for h in range(HEADS):
        qh = jnp.dot(x, wq_ref[:, h * 256:(h + 1) * 256], preferred_element_type=F32)
        q_abs = jnp.dot(qh[:, :QK_NOPE].astype(BF16), wn_ref[h], preferred_element_type=F32) * ATTN_SCALE
        q_pe = _rope_from_pair(qh[:, QK_NOPE:], cs) * ATTN_SCALE
        o_ref[:, h, :, :KV_RANK] = q_abs.astype(BF16).reshape(groups, QBLOCK, KV_RANK)
        o_ref[:, h, :, KV_RANK:] = q_pe.astype(BF16).reshape(groups, QBLOCK, QK_ROPE)


def mm_q(qn, w_q, w_nope, cs):
    m, k = qn.shape
    tm = _tile(m, 512)
    return pl.pallas_call(
        _mm_q_kernel,
        grid=(m // tm,),
        in_specs=[pl.BlockSpec((tm, k), lambda i: (i, 0)),
                  pl.BlockSpec(w_q.shape, lambda i: (0, 0)),
                  pl.BlockSpec(w_nope.shape, lambda i: (0, 0, 0)),
                  pl.BlockSpec((tm, LANES), lambda i: (i, 0))],
        out_specs=pl.BlockSpec((tm // QBLOCK, HEADS, QBLOCK, K_CAT), lambda i: (i, 0, 0, 0)),
        out_shape=jax.ShapeDtypeStruct((m // QBLOCK, HEADS, QBLOCK, K_CAT), BF16),
        compiler_params=_params(1),
        name="mm_q",
    )(qn, w_q, w_nope, cs)


def _mm_glu_kernel(x_ref, wa_ref, wb_ref, o_ref):
    x = x_ref[...]
    a = jnp.dot(x, wa_ref[...], preferred_element_type=F32)
    b = jnp.dot(x, wb_ref[...], preferred_element_type=F32)
    o_ref[...] = a * _sigmoid(b)


def mm_glu(x, w, width):
    m, k = x.shape
    tm = _tile(m, 1024)
    tn = _tile(width, 512)
    nb = width // tn
    return pl.pallas_call(
        _mm_glu_kernel,
        grid=(m // tm, nb),
        in_specs=[pl.BlockSpec((tm, k), lambda i, j: (i, 0)),
                  pl.BlockSpec((k, tn), lambda i, j: (0, j)),
                  pl.BlockSpec((k, tn), lambda i, j: (0, j + nb))],
        out_specs=pl.BlockSpec((tm, tn), lambda i, j: (i, j)),
        out_shape=jax.ShapeDtypeStruct((m, width), F32),
        compiler_params=_params(2),
        name="mm_glu",
    )(x, w, w)


def _mix_a_kernel(gu_ref, gv_ref, sz_ref, lg_ref, lb_ref, wm_ref, bias_ref, oa_ref, vo_ref):
    vn = _layer_norm(gv_ref[...], lg_ref[...], lb_ref[...])
    vo_ref[...] = vn
    vb = vn.astype(BF16)
    for g in range(A_GROUPS):
        sl = slice(g * A_GROUP_DIM, (g + 1) * A_GROUP_DIM)
        mix = jnp.dot(wm_ref[g], vb[:, sl], preferred_element_type=F32) + bias_ref[:, sl]
        oa_ref[:, sl] = (gu_ref[:, sl].astype(F32) * mix * sz_ref[:, sl].astype(F32)).astype(BF16)


def _softmax_update(s, v, m_ref, l_ref, acc_ref):
    n_rep = s.shape[1] // LANES
    m_prev = m_ref[...]
    m_new = jnp.maximum(m_prev, jnp.max(s, axis=1, keepdims=True))
    alpha = jnp.exp(m_prev - m_new)
    p = jnp.exp(s - jnp.concatenate([m_new] * n_rep, axis=1))
    l_ref[...] = alpha * l_ref[...] + jnp.sum(p, axis=1, keepdims=True)
    m_ref[...] = m_new
    acc_ref[...] = (acc_ref[...] * jnp.concatenate([alpha] * (KV_RANK // LANES), axis=1)
                    + jnp.dot(p.astype(BF16), v, preferred_element_type=F32))


def _softmax_init(m_ref, l_ref, acc_ref):
    m_ref[...] = jnp.full(m_ref.shape, NEG_INF, F32)
    l_ref[...] = jnp.zeros(l_ref.shape, F32)
    acc_ref[...] = jnp.zeros(acc_ref.shape, F32)


def _softmax_result(l_ref, acc_ref):
    inv = 1.0 / l_ref[...]
    return acc_ref[...] * jnp.concatenate([inv] * (KV_RANK // LANES), axis=1)


def _nt_dot(a, b):
    return lax.dot_general(a, b, (((1,), (1,)), ((), ())), preferred_element_type=F32)


def _prefill_kernel(q_ref, k_ref, sz_ref, wv_ref, o_ref, m_ref, l_ref, acc_ref, *, kc):
    qb = pl.program_id(1)
    rows = HEADS * QBLOCK
    q = q_ref[0].reshape(rows, K_CAT)
    _softmax_init(m_ref, l_ref, acc_ref)

    def chunk(k0, masked):
        k = k_ref[0, pl.ds(k0, kc), :]
        s = _nt_dot(q, k)
        if masked:
            q_pos = qb * QBLOCK + (lax.broadcasted_iota(jnp.int32, (rows, kc), 0) & (QBLOCK - 1))
            k_pos = k0 + lax.broadcasted_iota(jnp.int32, (rows, kc), 1)
            s = jnp.where(k_pos <= q_pos, s, NEG_INF)
        _softmax_update(s, k[:, :KV_RANK], m_ref, l_ref, acc_ref)

    n_full = (qb * QBLOCK) // kc

    def body(i, carry):
        chunk(pl.multiple_of(i * kc, kc), False)
        return carry

    lax.fori_loop(0, n_full, body, 0)
    chunk(pl.multiple_of(n_full * kc, kc), True)

    o = _softmax_result(l_ref, acc_ref)
    for h in range(HEADS):
        sl = slice(h * V_DIM, (h + 1) * V_DIM)
        r = jnp.dot(o[h * QBLOCK:(h + 1) * QBLOCK].astype(BF16), wv_ref[h], preferred_element_type=F32)
        o_ref[0, :, sl] = (r * sz_ref[0, :, sl].astype(F32)).astype(BF16)


def mla_prefill(q, kcat, sz, w_v, zb_col_block):
    b, l, _ = kcat.shape
    nq = l // QBLOCK
    kc = _tile(l, 512)
    rows = HEADS * QBLOCK
    return pl.pallas_call(
        functools.partial(_prefill_kernel, kc=kc),
        grid=(b, nq),
        in_specs=[pl.BlockSpec((1, HEADS, QBLOCK, K_CAT), lambda i, j: (i * nq + j, 0, 0, 0)),
                  pl.BlockSpec((1, l, K_CAT), lambda i, j: (i, 0, 0)),
                  pl.BlockSpec((1, QBLOCK, W_B), lambda i, j: (i, j, zb_col_block)),
                  pl.BlockSpec(w_v.shape, lambda i, j: (0, 0, 0))],
        out_specs=pl.BlockSpec((1, QBLOCK, W_B), lambda i, j: (i, j, 0)),
        out_shape=jax.ShapeDtypeStruct((b, l, W_B), BF16),
        scratch_shapes=[pltpu.VMEM((rows, LANES), F32), pltpu.VMEM((rows, LANES), F32),
                        pltpu.VMEM((rows, KV_RANK), F32)],
        compiler_params=_params(2),
        name="mla_prefill",
    )(q, kcat, sz, w_v)


def _decode_kernel(pt_ref, q_ref, knew_ref, *refs, pps, lq):
    c_refs = refs[:pps]
    r_refs = refs[pps:2 * pps]
    o_ref, cb_ref, rb_ref, m_ref, l_ref, acc_ref = refs[2 * pps:]
    step = pl.program_id(1)

    @pl.when(step == 0)
    def _():
        _softmax_init(m_ref, l_ref, acc_ref)

    for p in range(pps):
        cb_ref[p * PAGE:(p + 1) * PAGE, :] = c_refs[p][0, 0].astype(BF16)
        rb_ref[p * PAGE:(p + 1) * PAGE, :] = r_refs[p][0, 0].astype(BF16)
    q = q_ref[0]
    cb = cb_ref[...]
    s = _nt_dot(q[:, :KV_RANK], cb) + _nt_dot(q[:, KV_RANK:], rb_ref[...])
    _softmax_update(s, cb, m_ref, l_ref, acc_ref)

    @pl.when(step == pl.num_programs(1) - 1)
    def _():
        kn = knew_ref[0]
        rows = q.shape[0]
        s_new = _nt_dot(q, kn)
        q_pos = lax.broadcasted_iota(jnp.int32, (rows, PAGE), 0) & (lq - 1)
        k_pos = lax.broadcasted_iota(jnp.int32, (rows, PAGE), 1)
        s_new = jnp.where(k_pos <= q_pos, s_new, NEG_INF)
        _softmax_update(s_new, kn[:, :KV_RANK], m_ref, l_ref, acc_ref)
        o_ref[0] = _softmax_result(l_ref, acc_ref).astype(BF16)


def mla_decode(q, knew, cache_c, cache_r, layer, page_table):
    bd, rows, _ = q.shape
    lq = rows // HEADS
    assert lq & (lq - 1) == 0
    n_pages = page_table.shape[1]
    pps = 16 if n_pages % 16 == 0 else n_pages
    steps = n_pages // pps

    def page_map(p, width):
        return pl.BlockSpec((1, 1, PAGE, width),
                            lambda b, s, pt: (layer, pt[b * n_pages + s * pps + p], 0, 0))

    grid_spec = pltpu.PrefetchScalarGridSpec(
        num_scalar_prefetch=1,
        grid=(bd, steps),
        in_specs=([pl.BlockSpec((1, rows, K_CAT), lambda b, s, pt: (b, 0, 0)),
                   pl.BlockSpec((1, PAGE, K_CAT), lambda b, s, pt: (b, 0, 0))]
                  + [page_map(p, KV_RANK) for p in range(pps)]
                  + [page_map(p, QK_ROPE) for p in range(pps)]),
        out_specs=pl.BlockSpec((1, rows, KV_RANK), lambda b, s, pt: (b, 0, 0)),
        scratch_shapes=[pltpu.VMEM((pps * PAGE, KV_RANK), BF16), pltpu.VMEM((pps * PAGE, QK_ROPE), BF16),
                        pltpu.VMEM((rows, LANES), F32), pltpu.VMEM((rows, LANES), F32),
                        pltpu.VMEM((rows, KV_RANK), F32)],
    )
    return pl.pallas_call(
        functools.partial(_decode_kernel, pps=pps, lq=lq),
        grid_spec=grid_spec,
        out_shape=jax.ShapeDtypeStruct((bd, rows, KV_RANK), BF16),
        compiler_params=_params_seq(("parallel", "arbitrary")),
        name="mla_decode",
    )(page_table.reshape(-1), q, knew, *([cache_c] * pps), *([cache_r] * pps))


def _v_up_kernel(o_ref, wv_ref, sz_ref, out_ref):
    r = jnp.dot(o_ref[0], wv_ref[0], preferred_element_type=F32)
    out_ref[...] = (r * sz_ref[...].astype(F32)).astype(BF16)


def v_up(o_heads, w_v, sz, zb_col0):
    _, m, _ = o_heads.shape
    jb = zb_col0 // V_DIM
    return pl.pallas_call(
        _v_up_kernel,
        grid=(HEADS,),
        in_specs=[pl.BlockSpec((1, m, KV_RANK), lambda h: (h, 0, 0)),
                  pl.BlockSpec((1, KV_RANK, V_DIM), lambda h: (h, 0, 0)),
                  pl.BlockSpec((m, V_DIM), lambda h: (0, h + jb))],
        out_specs=pl.BlockSpec((m, V_DIM), lambda h: (0, h)),
        out_shape=jax.ShapeDtypeStruct((m, W_B), BF16),
        compiler_params=_params(1),
        name="v_up",
    )(o_heads, w_v, sz)


def _conv_kernel(prev_ref, g_ref, sz_ref, w_ref, b_ref, lg_ref, lb_ref, o_ref, xp_ref, d_ref,
                 *, tl, prev_rows, zero_first, lane_chunk, row_chunk):
    c = g_ref.shape[-1]
    prev = prev_ref[0]
    if zero_first:
        prev = jnp.where(pl.program_id(1) == 0, 0.0, prev)
    xp_ref[CONV_HALO - prev_rows:CONV_HALO, :] = prev
    xp_ref[CONV_HALO:CONV_HALO + tl, :] = g_ref[0]
    base = CONV_HALO - (CONV_W - 1)

    def lane_body(ci, carry):
        c0 = pl.multiple_of(ci * lane_chunk, lane_chunk)
        lanes = pl.ds(c0, lane_chunk)
        for r0 in range(0, tl, row_chunk):
            acc = jnp.broadcast_to(b_ref[:, lanes], (row_chunk, lane_chunk))
            for k in range(CONV_W):
                acc = acc + w_ref[k:k + 1, lanes] * xp_ref[base + r0 + k:base + r0 + k + row_chunk, lanes]
            d_ref[r0:r0 + row_chunk, lanes] = acc
        return carry

    lax.fori_loop(0, c // lane_chunk, lane_body, 0)
    y = _layer_norm(d_ref[...], lg_ref[...], lb_ref[...])
    o_ref[0] = (_silu(y) * sz_ref[0].astype(F32)).astype(BF16)


def conv_module(g, prev, sz, w_dw, b_dw, ln_g, ln_b, *, tl, zero_first):
    b, l, c = g.shape
    nl = l // tl
    if zero_first:
        prev_rows = CONV_HALO
        per = tl // CONV_HALO
        prev_spec = pl.BlockSpec((1, CONV_HALO, c), lambda i, j: (i, jnp.maximum(j * per - 1, 0), 0))
    else:
        assert nl == 1
        prev_rows = prev.shape[1]
        prev_spec = pl.BlockSpec((1, prev_rows, c), lambda i, j: (i, 0, 0))
    row_chunk = min(tl, 32)
    vec = lambda i, j: (0, 0)
    return pl.pallas_call(
        functools.partial(_conv_kernel, tl=tl, prev_rows=prev_rows, zero_first=zero_first,
                          lane_chunk=min(c, 256), row_chunk=row_chunk),
        grid=(b, nl),
        in_specs=[prev_spec,
                  pl.BlockSpec((1, tl, c), lambda i, j: (i, j, 0)),
                  pl.BlockSpec((1, tl, c), lambda i, j: (i, j, 0)),
                  pl.BlockSpec((CONV_W, c), vec), pl.BlockSpec((1, c), vec),
                  pl.BlockSpec((1, c), vec), pl.BlockSpec((1, c), vec)],
        out_specs=pl.BlockSpec((1, tl, c), lambda i, j: (i, j, 0)),
        out_shape=jax.ShapeDtypeStruct((b, l, c), BF16),
        scratch_shapes=[pltpu.VMEM((CONV_HALO + tl, c), F32), pltpu.VMEM((tl, c), F32)],
        compiler_params=_params(2),
        name="conv_module",
    )(prev, g, sz, w_dw, b_dw.reshape(1, c), ln_g.reshape(1, c), ln_b.reshape(1, c))


def _rope_cs(length, offset, reps):
    inv = jnp.power(ROPE_THETA, -jnp.arange(0, QK_ROPE, 2, dtype=F32) / QK_ROPE)
    pos = jnp.arange(length, dtype=F32) + offset
    ang = pos[:, None] * inv[None, :]
    cos, sin = jnp.cos(ang), jnp.sin(ang)
    return jnp.tile(jnp.concatenate([cos, cos, -sin, sin], axis=-1), (reps, 1))


def _swap_halves(w):
    half = w.shape[-1] // 2
    return jnp.concatenate([w[..., half:], w[..., :half]], axis=-1)


def _even_weights(w_in, w_q_b, w_kv_b, w_out):
    o_u, o_v, o_za = 0, W_A, 2 * W_A
    o_qa = o_za + W_A
    o_kv = o_qa + Q_RANK
    o_kr = o_kv + KV_RANK
    o_zb = o_kr + QK_ROPE
    w_kr = w_in[:, o_kr:o_zb]
    w_gate = jnp.concatenate([w_in[:, o_u:o_v], w_in[:, o_za:o_qa], w_in[:, o_zb:]], axis=1).astype(BF16)
    w_v = w_in[:, o_v:o_za].astype(BF16)
    w_qa = w_in[:, o_qa:o_kv].astype(BF16)
    w_kv = jnp.concatenate([w_in[:, o_kv:o_kr], w_kr, _swap_halves(w_kr)], axis=1).astype(BF16)
    wq = w_q_b.reshape(Q_RANK, HEADS, QK_NOPE + QK_ROPE)
    wq_rope = wq[..., QK_NOPE:]
    w_q = jnp.concatenate([wq[..., :QK_NOPE], wq_rope, _swap_halves(wq_rope)], axis=-1)
    w_q = w_q.reshape(Q_RANK, HEADS * 256).astype(BF16)
    wkvb = w_kv_b.reshape(KV_RANK, HEADS, QK_NOPE + V_DIM)
    w_nope = jnp.transpose(wkvb[..., :QK_NOPE], (1, 2, 0)).astype(BF16)
    w_vup = jnp.transpose(wkvb[..., QK_NOPE:], (1, 0, 2)).astype(BF16)
    return w_gate, w_v, w_qa, w_kv, w_q, w_nope, w_vup, w_out.astype(BF16)


def _spatial_weights(w_s, b_s, seq_len):
    cl = min(CHUNK, seq_len)
    reps = CHUNK // cl
    wm = (w_s * jnp.tril(jnp.ones((CHUNK, CHUNK), w_s.dtype)))[:, :cl, :cl]
    eye = jnp.eye(reps, dtype=w_s.dtype)
    wm = jnp.einsum("ab,gts->gatbs", eye, wm).reshape(A_GROUPS, CHUNK, CHUNK).astype(BF16)
    bias = jnp.tile(b_s[:, :cl].T, (reps, 1))
    bias = jnp.repeat(bias, A_GROUP_DIM, axis=1)
    return wm, bias


def _even_layer(x, batch, seq_len, offset, ew, ln_v_g, ln_v_b, w_s, b_s, g_q_a, g_kv_a, g_pre, g_post, past):
    w_gate, w_v, w_qa, w_kv, w_q, w_nope, w_vup, w_out = ew
    m = x.shape[0]
    h = rmsnorm_cast(x, g_pre)
    gates = mm_act(h, w_gate, col0=0, ncols=3 * W_A, n_gelu_cols=W_A, out_dtype=BF16)
    gv = mm_act(h, w_v, col0=0, ncols=W_A, n_gelu_cols=W_A, out_dtype=F32)
    qn = mm_rms(h, w_qa, g_q_a)
    cs = _rope_cs(seq_len, offset, batch)
    c, k_pe, kcat = mm_kv(h, w_kv, g_kv_a, cs)

    wm, bias = _spatial_weights(w_s, b_s, seq_len)
    blocks_per_seq = max(seq_len // CHUNK, 1)
    out_a, v_open = _mix_a_call(gates, gv, ln_v_g, ln_v_b, wm, bias, blocks_per_seq)

    q = mm_q(qn, w_q, w_nope, cs)
    if past is None:
        out_b = mla_prefill(q, kcat.reshape(batch, seq_len, K_CAT), gates.reshape(batch, seq_len, 3 * W_A),
                            w_vup, zb_col_block=2).reshape(m, W_B)
    else:
        cache_c, cache_r, layer, page_table = past
        per = QBLOCK // seq_len
        qd = q.reshape(m // QBLOCK, HEADS, per, seq_len, K_CAT)
        qd = jnp.transpose(qd, (0, 2, 1, 3, 4)).reshape(batch, HEADS * seq_len, K_CAT)
        knew = jnp.pad(kcat.reshape(batch, seq_len, K_CAT), ((0, 0), (0, PAGE - seq_len), (0, 0)))
        o_lat = mla_decode(qd, knew, cache_c, cache_r, layer, page_table)
        o_heads = jnp.transpose(o_lat.reshape(batch, HEADS, seq_len, KV_RANK), (1, 0, 2, 3))
        out_b = v_up(o_heads.reshape(HEADS, m, KV_RANK), w_vup, gates, zb_col0=2 * W_A)
    y = mm_plain([out_a, out_b], w_out)
    return residual_norm(x, y, g_post), c, k_pe, v_open


def _mix_a_call(gates, gv, ln_v_g, ln_v_b, wm, bias, blocks_per_seq):
    m = gv.shape[0]
    nblk = m // CHUNK
    row0 = lambda i: (i, 0)
    row1 = lambda i: (i, 1)
    fixed2 = lambda i: (0, 0)
    return pl.pallas_call(
        _mix_a_kernel,
        grid=(nblk,),
        in_specs=[pl.BlockSpec((CHUNK, W_A), row0), pl.BlockSpec((CHUNK, W_A), row0),
                  pl.BlockSpec((CHUNK, W_A), row1),
                  pl.BlockSpec((1, W_A), fixed2), pl.BlockSpec((1, W_A), fixed2),
                  pl.BlockSpec((A_GROUPS, CHUNK, CHUNK), lambda i: (0, 0, 0)),
                  pl.BlockSpec((CHUNK, W_A), fixed2)],
        out_specs=[pl.BlockSpec((CHUNK, W_A), row0),
                   pl.BlockSpec((CHUNK, W_A), lambda i: (i // blocks_per_seq, 0))],
        out_shape=[jax.ShapeDtypeStruct((m, W_A), BF16),
                   jax.ShapeDtypeStruct((m // blocks_per_seq, W_A), F32)],
        compiler_params=_params_seq(("arbitrary",)),
        name="mix_a",
    )(gates, gv, gates, ln_v_g.reshape(1, W_A), ln_v_b.reshape(1, W_A), wm, bias)


def _odd_layer(x, batch, seq_len, conv_prev, w_in, w_dw, b_dw, ln_c_g, ln_c_b, w_out, g_pre, g_post):
    m, d = x.shape
    c = w_dw.shape[1]
    h = rmsnorm_cast(x, g_pre)
    g = mm_glu(h, w_in, c).reshape(batch, seq_len, c)
    sz = mm_act(h, w_in, col0=2 * c, ncols=c, n_gelu_cols=0, out_dtype=BF16).reshape(batch, seq_len, c)
    if conv_prev is None:
        y2 = conv_module(g, g, sz, w_dw, b_dw, ln_c_g, ln_c_b, tl=min(seq_len, 128), zero_first=True)
    else:
        y2 = conv_module(g, conv_prev, sz, w_dw, b_dw, ln_c_g, ln_c_b, tl=seq_len, zero_first=False)
    y = mm_plain([y2.reshape(m, c)], w_out)
    return residual_norm(x, y, g_post), g


def kernel(x_prompt, x_sample, cache_kv_latent, cache_k_rope, state_conv, page_table, w_in_even, ln_v_g, ln_v_b,
           w_s, b_s, g_q_a, w_q_b, g_kv_a, w_kv_b, w_out_even, w_in_odd, w_dw, b_dw, ln_c_g, ln_c_b, w_out_odd,
           g_pre, g_post):
    bp, lp, d = x_prompt.shape
    bs, ls, _ = x_sample.shape
    past_len = page_table.shape[1] * PAGE
    depth = g_pre.shape[0]
    xp = x_prompt.reshape(bp * lp, d)
    xs = x_sample.reshape(bs * ls, d)
    lat_p, kpe_p, v_p, conv_p = [], [], [], []
    lat_s, kpe_s, v_s, conv_s = [], [], [], []
    for i in range(depth):
        j = i // 2
        if i % 2 == 0:
            ew = _even_weights(w_in_even[j], w_q_b[j], w_kv_b[j], w_out_even[j])
            args = (ew, ln_v_g[j], ln_v_b[j], w_s[j], b_s[j], g_q_a[j], g_kv_a[j], g_pre[i], g_post[i])
            xp, c, k, v = _even_layer(xp, bp, lp, 0, *args, None)
            lat_p.append(c.reshape(bp, lp, KV_RANK))
            kpe_p.append(k.reshape(bp, lp, QK_ROPE))
            v_p.append(v.reshape(bp, -1, W_A))
            xs, c, k, v = _even_layer(xs, bs, ls, past_len, *args,
                                      (cache_kv_latent, cache_k_rope, j, page_table))
            lat_s.append(c.reshape(bs, ls, KV_RANK))
            kpe_s.append(k.reshape(bs, ls, QK_ROPE))
            v_s.append(v.reshape(bs, ls, W_A))
        else:
            w_in = w_in_odd[j].astype(BF16)
            args = (w_in, w_dw[j], b_dw[j], ln_c_g[j], ln_c_b[j], w_out_odd[j].astype(BF16), g_pre[i], g_post[i])
            xp, g = _odd_layer(xp, bp, lp, None, *args)
            conv_p.append(g[:, lp - (CONV_W - 1):])
            xs, g = _odd_layer(xs, bs, ls, state_conv[j], *args)
            conv_s.append(jnp.concatenate([state_conv[j], g], axis=1)[:, -(CONV_W - 1):])
    return (xp.reshape(bp, lp, d), xs.reshape(bs, ls, d), jnp.stack(lat_p), jnp.stack(kpe_p), jnp.stack(v_p),
            jnp.stack(conv_p), jnp.stack(lat_s), jnp.stack(kpe_s), jnp.stack(v_s), jnp.stack(conv_s))
```

```python
import functools
import math

import jax
import jax.numpy as jnp
from jax import lax
from jax.experimental import pallas as pl
from jax.experimental.pallas import tpu as pltpu

F32 = jnp.float32
BF16 = jnp.bfloat16

EPS = 1e-6
LANES = 128
SUBLANES = 8
CHUNK = 128
A_GROUPS = 16
A_GROUP_DIM = 128
W_A = A_GROUPS * A_GROUP_DIM
HEADS = 16
QK_NOPE = 128
QK_ROPE = 64
V_DIM = 128
Q_RANK = 1024
KV_RANK = 512
W_B = HEADS * V_DIM
K_CAT = KV_RANK + QK_ROPE
ROPE_THETA = 10000.0
QBLOCK = 128
PAGE = 128
DECODE_PAGES_PER_STEP = 32
DECODE_SEQS_PER_STEP = 1
CONV_W = 31
CONV_HALO = 32
LN_ROWS = 16
ATTN_SCALE = 1.0 / math.sqrt(QK_NOPE + QK_ROPE)
NEG_INF = float("-inf")
VMEM_LIMIT_BYTES = 56 * 1024 * 1024


def _params(n_axes):
    return pltpu.CompilerParams(dimension_semantics=("parallel",) * n_axes,
                                vmem_limit_bytes=VMEM_LIMIT_BYTES)


def _params_seq(sem):
    return pltpu.CompilerParams(dimension_semantics=sem, vmem_limit_bytes=VMEM_LIMIT_BYTES)


def _sigmoid(x):
    return 1.0 / (1.0 + jnp.exp(-x))


def _silu(x):
    return x * _sigmoid(x)


def _rms(x, g):
    return x * lax.rsqrt(jnp.mean(x * x, axis=-1, keepdims=True) + EPS) * g


def _layer_norm(x, g, b):
    mu = jnp.mean(x, axis=-1, keepdims=True)
    xc = x - mu
    return xc * lax.rsqrt(jnp.mean(xc * xc, axis=-1, keepdims=True) + EPS) * g + b


def _tile(n, pref):
    return pref if n % pref == 0 else n


def _rmsnorm_kernel(x_ref, g_ref, o_ref):
    o_ref[...] = _rms(x_ref[...], g_ref[...]).astype(o_ref.dtype)


def rmsnorm_cast(x, g):
    m, d = x.shape
    tm = _tile(m, 256)
    return pl.pallas_call(
        _rmsnorm_kernel,
        grid=(m // tm,),
        in_specs=[pl.BlockSpec((tm, d), lambda i: (i, 0)), pl.BlockSpec((1, d), lambda i: (0, 0))],
        out_specs=pl.BlockSpec((tm, d), lambda i: (i, 0)),
        out_shape=jax.ShapeDtypeStruct((m, d), BF16),
        compiler_params=_params(1),
        name="rmsnorm_cast",
    )(x, g.reshape(1, d))


def _residual_norm_kernel(x_ref, y_ref, g_ref, o_ref):
    o_ref[...] = x_ref[...] + _rms(y_ref[...], g_ref[...])


def residual_norm(x, y, g):
    m, d = x.shape
    tm = _tile(m, 256)
    return pl.pallas_call(
        _residual_norm_kernel,
        grid=(m // tm,),
        in_specs=[pl.BlockSpec((tm, d), lambda i: (i, 0)), pl.BlockSpec((tm, d), lambda i: (i, 0)),
                  pl.BlockSpec((1, d), lambda i: (0, 0))],
        out_specs=pl.BlockSpec((tm, d), lambda i: (i, 0)),
        out_shape=jax.ShapeDtypeStruct((m, d), F32),
        compiler_params=_params(1),
        name="residual_norm",
    )(x, y, g.reshape(1, d))


def _bf16_dot(x, w, transposed_w=False):
    w = w.astype(BF16)
    if transposed_w:
        return _nt_dot(x, w)
    return jnp.dot(x, w, preferred_element_type=F32)


def _mm_act_kernel(x_ref, w_ref, o_ref, *, n_gelu_tiles, transposed_w):
    acc = _bf16_dot(x_ref[...], w_ref[...], transposed_w)
    j = pl.program_id(1)

    @pl.when(j < n_gelu_tiles)
    def _():
        o_ref[...] = jax.nn.gelu(acc).astype(o_ref.dtype)

    @pl.when(j >= n_gelu_tiles)
    def _():
        o_ref[...] = _silu(acc).astype(o_ref.dtype)


def mm_act(x, w, *, col0, ncols, n_gelu_cols, out_dtype, transposed_w=False):
    m, k = x.shape
    tm = _tile(m, 1024)
    tn = _tile(ncols, 512)
    assert col0 % tn == 0 and n_gelu_cols % tn == 0
    j0 = col0 // tn
    if transposed_w:
        w_spec = pl.BlockSpec((tn, k), lambda i, j: (j + j0, 0))
    else:
        w_spec = pl.BlockSpec((k, tn), lambda i, j: (0, j + j0))
    return pl.pallas_call(
        functools.partial(_mm_act_kernel, n_gelu_tiles=n_gelu_cols // tn, transposed_w=transposed_w),
        grid=(m // tm, ncols // tn),
        in_specs=[pl.BlockSpec((tm, k), lambda i, j: (i, 0)), w_spec],
        out_specs=pl.BlockSpec((tm, tn), lambda i, j: (i, j)),
        out_shape=jax.ShapeDtypeStruct((m, ncols), out_dtype),
        compiler_params=_params(2),
        name="mm_act",
    )(x, w)


def _mm_plain_kernel(*refs, n_in):
    x_refs = refs[:n_in]
    w_refs = refs[n_in:2 * n_in]
    o_ref = refs[2 * n_in]
    acc = _bf16_dot(x_refs[0][...], w_refs[0][...])
    for x_ref, w_ref in zip(x_refs[1:], w_refs[1:]):
        acc += _bf16_dot(x_ref[...], w_ref[...])
    o_ref[...] = acc


def mm_plain(xs, w):
    m = xs[0].shape[0]
    n = w.shape[1]
    tm = _tile(m, 1024)
    tn = _tile(n, 512)
    in_specs = [pl.BlockSpec((tm, x.shape[1]), lambda i, j: (i, 0)) for x in xs]
    row = 0
    for x in xs:
        kx = x.shape[1]
        assert row % kx == 0
        in_specs.append(pl.BlockSpec((kx, tn), functools.partial(lambda i, j, r: (r, j), r=row // kx)))
        row += kx
    assert row == w.shape[0]
    return pl.pallas_call(
        functools.partial(_mm_plain_kernel, n_in=len(xs)),
        grid=(m // tm, n // tn),
        in_specs=in_specs,
        out_specs=pl.BlockSpec((tm, tn), lambda i, j: (i, j)),
        out_shape=jax.ShapeDtypeStruct((m, n), F32),
        compiler_params=_params(2),
        name="mm_plain",
    )(*xs, *([w] * len(xs)))


def _mm_rms_kernel(x_ref, w_ref, g_ref, o_ref):
    acc = _nt_dot(x_ref[...], w_ref[...])
    o_ref[...] = _rms(acc, g_ref[...]).astype(o_ref.dtype)


def mm_rms(x, w_t, g):
    m, k = x.shape
    n = w_t.shape[0]
    tm = _tile(m, 1024)
    return pl.pallas_call(
        _mm_rms_kernel,
        grid=(m // tm,),
        in_specs=[pl.BlockSpec((tm, k), lambda i: (i, 0)), pl.BlockSpec((n, k), lambda i: (0, 0)),
                  pl.BlockSpec((1, n), lambda i: (0, 0))],
        out_specs=pl.BlockSpec((tm, n), lambda i: (i, 0)),
        out_shape=jax.ShapeDtypeStruct((m, n), BF16),
        compiler_params=_params(1),
        name="mm_rms",
    )(x, w_t, g.reshape(1, n))


def _rope_from_pair(pair, cs):
    t = pair * cs
    return (t + pltpu.roll(t, QK_ROPE, axis=1))[:, :QK_ROPE]


def _mm_kv_kernel(x_ref, w_ref, g_ref, cs_ref, c_ref, kpe_ref, kcat_ref):
    acc = _nt_dot(x_ref[...], w_ref[...])
    c = _rms(acc[:, :KV_RANK], g_ref[...])
    kpe = _rope_from_pair(acc[:, KV_RANK:], cs_ref[...])
    c_ref[...] = c
    kpe_ref[...] = kpe
    kcat_ref[:, :KV_RANK] = c.astype(BF16)
    kcat_ref[:, KV_RANK:] = kpe.astype(BF16)


def mm_kv(x, w_kv, g_kv, cs):
    m, k = x.shape
    n = w_kv.shape[0]
    tm = _tile(m, 1024)
    return pl.pallas_call(
        _mm_kv_kernel,
        grid=(m // tm,),
        in_specs=[pl.BlockSpec((tm, k), lambda i: (i, 0)), pl.BlockSpec((n, k), lambda i: (0, 0)),
                  pl.BlockSpec((1, KV_RANK), lambda i: (0, 0)), pl.BlockSpec((tm, LANES), lambda i: (i, 0))],
        out_specs=[pl.BlockSpec((tm, KV_RANK), lambda i: (i, 0)), pl.BlockSpec((tm, QK_ROPE), lambda i: (i, 0)),
                   pl.BlockSpec((tm, K_CAT), lambda i: (i, 0))],
        out_shape=[jax.ShapeDtypeStruct((m, KV_RANK), F32), jax.ShapeDtypeStruct((m, QK_ROPE), F32),
                   jax.ShapeDtypeStruct((m, K_CAT), BF16)],
        compiler_params=_params(1),
        name="mm_kv",
    )(x, w_kv, g_kv.reshape(1, KV_RANK), cs)


def _mm_q_kernel(x_ref, wq_ref, wn_ref, cs_ref, o_ref):
    x = x_ref[...]
    cs = cs_ref[...]
    groups = x.shape[0] // QBLOCK
    for h in range(HEADS):
        qh = jnp.dot(x, wq_ref[:, h * 256:(h + 1) * 256], preferred_element_type=F32)
        q_abs = jnp.dot(qh[:, :QK_NOPE].astype(BF16), wn_ref[h], preferred_element_type=F32) * ATTN_SCALE
        q_pe = _rope_from_pair(qh[:, QK_NOPE:], cs) * ATTN_SCALE
        o_ref[:, h, :, :KV_RANK] = q_abs.astype(BF16).reshape(groups, QBLOCK, KV_RANK)
        o_ref[:, h, :, KV_RANK:] = q_pe.astype(BF16).reshape(groups, QBLOCK, QK_ROPE)


def mm_q(qn, w_q, w_nope, cs):
    m, k = qn.shape
    tm = _tile(m, 512)
    return pl.pallas_call(
        _mm_q_kernel,
        grid=(m // tm,),
        in_specs=[pl.BlockSpec((tm, k), lambda i: (i, 0)),
                  pl.BlockSpec(w_q.shape, lambda i: (0, 0)),
                  pl.BlockSpec(w_nope.shape, lambda i: (0, 0, 0)),
                  pl.BlockSpec((tm, LANES), lambda i: (i, 0))],
        out_specs=pl.BlockSpec((tm // QBLOCK, HEADS, QBLOCK, K_CAT), lambda i: (i, 0, 0, 0)),
        out_shape=jax.ShapeDtypeStruct((m // QBLOCK, HEADS, QBLOCK, K_CAT), BF16),
        compiler_params=_params(1),
        name="mm_q",
    )(qn, w_q, w_nope, cs)


def _mm_glu_kernel(x_ref, wa_ref, wb_ref, o_ref):
    x = x_ref[...]
    a = _bf16_dot(x, wa_ref[...])
    b = _bf16_dot(x, wb_ref[...])
    o_ref[...] = a * _sigmoid(b)


def mm_glu(x, w, width):
    m, k = x.shape
    tm = _tile(m, 1024)
    tn = _tile(width, 256)
    nb = width // tn
    return pl.pallas_call(
        _mm_glu_kernel,
        grid=(m // tm, nb),
        in_specs=[pl.BlockSpec((tm, k), lambda i, j: (i, 0)),
                  pl.BlockSpec((k, tn), lambda i, j: (0, j)),
                  pl.BlockSpec((k, tn), lambda i, j: (0, j + nb))],
        out_specs=pl.BlockSpec((tm, tn), lambda i, j: (i, j)),
        out_shape=jax.ShapeDtypeStruct((m, width), F32),
        compiler_params=_params(2),
        name="mm_glu",
    )(x, w, w)


def _mix_a_kernel(gu_ref, gv_ref, sz_ref, lg_ref, lb_ref, wm_ref, bias_ref, oa_ref, vo_ref):
    vn = _layer_norm(gv_ref[...], lg_ref[...], lb_ref[...])
    vo_ref[...] = vn
    vb = vn.astype(BF16)
    for g in range(A_GROUPS):
        sl = slice(g * A_GROUP_DIM, (g + 1) * A_GROUP_DIM)
        mix = jnp.dot(wm_ref[g], vb[:, sl], preferred_element_type=F32) + bias_ref[:, sl]
        oa_ref[:, sl] = (gu_ref[:, sl].astype(F32) * mix * sz_ref[:, sl].astype(F32)).astype(BF16)


def _softmax_update(s, v, m_ref, l_ref, acc_ref):
    n_rep = s.shape[1] // LANES
    m_prev = m_ref[...]
    m_new = jnp.maximum(m_prev, jnp.max(s, axis=1, keepdims=True))
    alpha = jnp.exp(m_prev - m_new)
    p = jnp.exp(s - jnp.concatenate([m_new] * n_rep, axis=1))
    l_ref[...] = alpha * l_ref[...] + jnp.sum(p, axis=1, keepdims=True)
    m_ref[...] = m_new
    acc_ref[...] = (acc_ref[...] * jnp.concatenate([alpha] * (KV_RANK // LANES), axis=1)
                    + jnp.dot(p.astype(BF16), v, preferred_element_type=F32))


def _softmax_init(m_ref, l_ref, acc_ref):
    m_ref[...] = jnp.full(m_ref.shape, NEG_INF, F32)
    l_ref[...] = jnp.zeros(l_ref.shape, F32)
    acc_ref[...] = jnp.zeros(acc_ref.shape, F32)


def _softmax_result(l_ref, acc_ref):
    inv = 1.0 / l_ref[...]
    return acc_ref[...] * jnp.concatenate([inv] * (KV_RANK // LANES), axis=1)


def _nt_dot(a, b):
    return lax.dot_general(a, b, (((1,), (1,)), ((), ())), preferred_element_type=F32)


def _prefill_kernel(q_ref, k_ref, sz_ref, wv_ref, o_ref, m_ref, l_ref, acc_ref, *, kc):
    qb = pl.program_id(1)
    rows = HEADS * QBLOCK
    q = q_ref[0].reshape(rows, K_CAT)
    _softmax_init(m_ref, l_ref, acc_ref)

    def chunk(k0, masked):
        k = k_ref[0, pl.ds(k0, kc), :]
        s = _nt_dot(q, k)
        if masked:
            q_pos = qb * QBLOCK + (lax.broadcasted_iota(jnp.int32, (rows, kc), 0) & (QBLOCK - 1))
            k_pos = k0 + lax.broadcasted_iota(jnp.int32, (rows, kc), 1)
            s = jnp.where(k_pos <= q_pos, s, NEG_INF)
        _softmax_update(s, k[:, :KV_RANK], m_ref, l_ref, acc_ref)

    n_full = (qb * QBLOCK) // kc

    def body(i, carry):
        chunk(pl.multiple_of(i * kc, kc), False)
        return carry

    lax.fori_loop(0, n_full, body, 0)
    chunk(pl.multiple_of(n_full * kc, kc), True)

    o = _softmax_result(l_ref, acc_ref)
    for h in range(HEADS):
        sl = slice(h * V_DIM, (h + 1) * V_DIM)
        r = jnp.dot(o[h * QBLOCK:(h + 1) * QBLOCK].astype(BF16), wv_ref[h], preferred_element_type=F32)
        o_ref[0, :, sl] = (r * sz_ref[0, :, sl].astype(F32)).astype(BF16)


def mla_prefill(q, kcat, sz, w_v, zb_col_block):
    b, l, _ = kcat.shape
    nq = l // QBLOCK
    kc = _tile(l, 512)
    rows = HEADS * QBLOCK
    return pl.pallas_call(
        functools.partial(_prefill_kernel, kc=kc),
        grid=(b, nq),
        in_specs=[pl.BlockSpec((1, HEADS, QBLOCK, K_CAT), lambda i, j: (i * nq + j, 0, 0, 0)),
                  pl.BlockSpec((1, l, K_CAT), lambda i, j: (i, 0, 0)),
                  pl.BlockSpec((1, QBLOCK, W_B), lambda i, j: (i, j, zb_col_block)),
                  pl.BlockSpec(w_v.shape, lambda i, j: (0, 0, 0))],
        out_specs=pl.BlockSpec((1, QBLOCK, W_B), lambda i, j: (i, j, 0)),
        out_shape=jax.ShapeDtypeStruct((b, l, W_B), BF16),
        scratch_shapes=[pltpu.VMEM((rows, LANES), F32), pltpu.VMEM((rows, LANES), F32),
                        pltpu.VMEM((rows, KV_RANK), F32)],
        compiler_params=_params(2),
        name="mla_prefill",
    )(q, kcat, sz, w_v)


def _decode_kernel(pt_ref, q_ref, knew_ref, *refs, pps, lq, seqs):
    n_pg = seqs * pps
    c_refs = refs[:n_pg]
    r_refs = refs[n_pg:2 * n_pg]
    o_ref, cb_ref, rb_ref, m_ref, l_ref, acc_ref = refs[2 * n_pg:]
    step = pl.program_id(1)

    @pl.when(step == 0)
    def _():
        _softmax_init(m_ref, l_ref, acc_ref)

    for e in range(seqs):
        for p in range(pps):
            cb_ref[e, p * PAGE:(p + 1) * PAGE, :] = c_refs[e * pps + p][0, 0].astype(BF16)
            rb_ref[e, :, p * PAGE:(p + 1) * PAGE] = r_refs[e * pps + p][0, 0].astype(BF16)
        q = q_ref[e]
        cb = cb_ref[e]
        s = _nt_dot(q[:, :KV_RANK], cb) + jnp.dot(q[:, KV_RANK:], rb_ref[e], preferred_element_type=F32)
        _softmax_update(s, cb, m_ref.at[e], l_ref.at[e], acc_ref.at[e])

    @pl.when(step == pl.num_programs(1) - 1)
    def _():
        for e in range(seqs):
            q = q_ref[e]
            kn = knew_ref[e]
            rows = q.shape[0]
            s_new = _nt_dot(q, kn)
            q_pos = lax.broadcasted_iota(jnp.int32, (rows, PAGE), 0) & (lq - 1)
            k_pos = lax.broadcasted_iota(jnp.int32, (rows, PAGE), 1)
            s_new = jnp.where(k_pos <= q_pos, s_new, NEG_INF)
            _softmax_update(s_new, kn[:, :KV_RANK], m_ref.at[e], l_ref.at[e], acc_ref.at[e])
            o_ref[e] = _softmax_result(l_ref.at[e], acc_ref.at[e]).astype(BF16)


def mla_decode(q, knew, cache_c, cache_rt, layer, page_table):
    bd, rows, _ = q.shape
    lq = rows // HEADS
    assert lq & (lq - 1) == 0
    n_pages = page_table.shape[1]
    pps = DECODE_PAGES_PER_STEP if n_pages % DECODE_PAGES_PER_STEP == 0 else n_pages
    seqs = DECODE_SEQS_PER_STEP if bd % DECODE_SEQS_PER_STEP == 0 else 1
    steps = n_pages // pps

    def page_map(e, p, shape):
        return pl.BlockSpec((1, 1) + shape,
                            lambda b, s, pt: (layer, pt[(b * seqs + e) * n_pages + s * pps + p], 0, 0))

    pages = [(e, p) for e in range(seqs) for p in range(pps)]
    grid_spec = pltpu.PrefetchScalarGridSpec(
        num_scalar_prefetch=1,
        grid=(bd // seqs, steps),
        in_specs=([pl.BlockSpec((seqs, rows, K_CAT), lambda b, s, pt: (b, 0, 0)),
                   pl.BlockSpec((seqs, PAGE, K_CAT), lambda b, s, pt: (b, 0, 0))]
                  + [page_map(e, p, (PAGE, KV_RANK)) for e, p in pages]
                  + [page_map(e, p, (QK_ROPE, PAGE)) for e, p in pages]),
        out_specs=pl.BlockSpec((seqs, rows, KV_RANK), lambda b, s, pt: (b, 0, 0)),
        scratch_shapes=[pltpu.VMEM((seqs, pps * PAGE, KV_RANK), BF16), pltpu.VMEM((seqs, QK_ROPE, pps * PAGE), BF16),
                        pltpu.VMEM((seqs, rows, LANES), F32), pltpu.VMEM((seqs, rows, LANES), F32),
                        pltpu.VMEM((seqs, rows, KV_RANK), F32)],
    )
    n_pg = seqs * pps
    return pl.pallas_call(
        functools.partial(_decode_kernel, pps=pps, lq=lq, seqs=seqs),
        grid_spec=grid_spec,
        out_shape=jax.ShapeDtypeStruct((bd, rows, KV_RANK), BF16),
        compiler_params=_params_seq(("parallel", "arbitrary")),
        name="mla_decode",
    )(page_table.reshape(-1), q, knew, *([cache_c] * n_pg), *([cache_rt] * n_pg))


def _v_up_kernel(o_ref, wv_ref, sz_ref, out_ref):
    r = jnp.dot(o_ref[0], wv_ref[0], preferred_element_type=F32)
    out_ref[...] = (r * sz_ref[...].astype(F32)).astype(BF16)


def v_up(o_heads, w_v, sz, zb_col0):
    _, m, _ = o_heads.shape
    jb = zb_col0 // V_DIM
    return pl.pallas_call(
        _v_up_kernel,
        grid=(HEADS,),
        in_specs=[pl.BlockSpec((1, m, KV_RANK), lambda h: (h, 0, 0)),
                  pl.BlockSpec((1, KV_RANK, V_DIM), lambda h: (h, 0, 0)),
                  pl.BlockSpec((m, V_DIM), lambda h: (0, h + jb))],
        out_specs=pl.BlockSpec((m, V_DIM), lambda h: (0, h)),
        out_shape=jax.ShapeDtypeStruct((m, W_B), BF16),
        compiler_params=_params(1),
        name="v_up",
    )(o_heads, w_v, sz)


def _conv_kernel(prev_ref, g_ref, sz_ref, w_ref, b_ref, lg_ref, lb_ref, o_ref, xp_ref, xs_ref, d_ref,
                 *, tl, prev_rows, zero_first, lane_chunk):
    c = g_ref.shape[-1]
    prev = prev_ref[0]
    if zero_first:
        prev = jnp.where(pl.program_id(1) == 0, 0.0, prev)
    xp_ref[CONV_HALO - prev_rows:CONV_HALO, :] = prev
    xp_ref[CONV_HALO:CONV_HALO + tl, :] = g_ref[0]
    base = CONV_HALO - (CONV_W - 1)

    def lane_body(ci, carry):
        c0 = pl.multiple_of(ci * lane_chunk, lane_chunk)
        lanes = pl.ds(c0, lane_chunk)
        for phase in range(1, SUBLANES):
            xs_ref[phase] = xp_ref[phase:phase + tl + CONV_HALO - SUBLANES, lanes]
        acc = jnp.broadcast_to(b_ref[:, lanes], (tl, lane_chunk))
        for k in range(CONV_W):
            phase = (base + k) % SUBLANES
            off = (base + k) - phase
            if phase == 0:
                x = xp_ref[off:off + tl, lanes]
            else:
                x = xs_ref[phase, off:off + tl, :]
            acc = acc + w_ref[k:k + 1, lanes] * x
        d_ref[:, lanes] = acc
        return carry

    lax.fori_loop(0, c // lane_chunk, lane_body, 0)

    rows = min(tl, LN_ROWS)

    def norm_body(ri, carry):
        r = pl.ds(pl.multiple_of(ri * rows, rows), rows)
        y = _layer_norm(d_ref[r, :], lg_ref[...], lb_ref[...])
        o_ref[0, r, :] = (_silu(y) * sz_ref[0, r, :].astype(F32)).astype(BF16)
        return carry

    lax.fori_loop(0, tl // rows, norm_body, 0)


def conv_module(g, prev, sz, w_dw, b_dw, ln_g, ln_b, *, tl, zero_first):
    b, l, c = g.shape
    nl = l // tl
    if zero_first:
        prev_rows = CONV_HALO
        per = tl // CONV_HALO
        prev_spec = pl.BlockSpec((1, CONV_HALO, c), lambda i, j: (i, jnp.maximum(j * per - 1, 0), 0))
    else:
        assert nl == 1
        prev_rows = prev.shape[1]
        prev_spec = pl.BlockSpec((1, prev_rows, c), lambda i, j: (i, 0, 0))
    vec = lambda i, j: (0, 0)
    lane_chunk = LANES if tl >= 64 else min(c, 4 * LANES)
    return pl.pallas_call(
        functools.partial(_conv_kernel, tl=tl, prev_rows=prev_rows, zero_first=zero_first, lane_chunk=lane_chunk),
        grid=(b, nl),
        in_specs=[prev_spec,
                  pl.BlockSpec((1, tl, c), lambda i, j: (i, j, 0)),
                  pl.BlockSpec((1, tl, c), lambda i, j: (i, j, 0)),
                  pl.BlockSpec((CONV_W, c), vec), pl.BlockSpec((1, c), vec),
                  pl.BlockSpec((1, c), vec), pl.BlockSpec((1, c), vec)],
        out_specs=pl.BlockSpec((1, tl, c), lambda i, j: (i, j, 0)),
        out_shape=jax.ShapeDtypeStruct((b, l, c), BF16),
        scratch_shapes=[pltpu.VMEM((CONV_HALO + tl, c), F32),
                        pltpu.VMEM((SUBLANES, tl + CONV_HALO - SUBLANES, lane_chunk), F32),
                        pltpu.VMEM((tl, c), F32)],
        compiler_params=_params(2),
        name="conv_module",
    )(prev, g, sz, w_dw, b_dw.reshape(1, c), ln_g.reshape(1, c), ln_b.reshape(1, c))


def _rope_cs(length, offset, reps):
    inv = jnp.power(ROPE_THETA, -jnp.arange(0, QK_ROPE, 2, dtype=F32) / QK_ROPE)
    pos = jnp.arange(length, dtype=F32) + offset
    ang = pos[:, None] * inv[None, :]
    cos, sin = jnp.cos(ang), jnp.sin(ang)
    return jnp.tile(jnp.concatenate([cos, cos, -sin, sin], axis=-1), (reps, 1))


def _swap_halves(w):
    half = w.shape[-1] // 2
    return jnp.concatenate([w[..., half:], w[..., :half]], axis=-1)


def _even_weights(w_in, w_q_b, w_kv_b):
    wt = w_in.T
    o_v, o_za = W_A, 2 * W_A
    o_qa = o_za + W_A
    o_kv = o_qa + Q_RANK
    o_kr = o_kv + KV_RANK
    o_zb = o_kr + QK_ROPE
    w_kr = wt[o_kr:o_zb]
    w_kr_sw = jnp.concatenate([w_kr[QK_ROPE // 2:], w_kr[:QK_ROPE // 2]], axis=0)
    w_gate = jnp.concatenate([wt[:o_v], wt[o_za:o_qa], wt[o_zb:]], axis=0).astype(BF16)
    w_qa = wt[o_qa:o_kv].astype(BF16)
    w_kv = jnp.concatenate([wt[o_kv:o_kr], w_kr, w_kr_sw], axis=0).astype(BF16)
    wq = w_q_b.reshape(Q_RANK, HEADS, QK_NOPE + QK_ROPE)
    wq_rope = wq[..., QK_NOPE:]
    w_q = jnp.concatenate([wq[..., :QK_NOPE], wq_rope, _swap_halves(wq_rope)], axis=-1)
    w_q = w_q.reshape(Q_RANK, HEADS * 256).astype(BF16)
    wkvb = w_kv_b.reshape(KV_RANK, HEADS, QK_NOPE + V_DIM)
    w_nope = jnp.transpose(wkvb[..., :QK_NOPE], (1, 2, 0)).astype(BF16)
    w_vup = jnp.transpose(wkvb[..., QK_NOPE:], (1, 0, 2)).astype(BF16)
    return wt, w_gate, w_qa, w_kv, w_q, w_nope, w_vup


def _spatial_weights(w_s, b_s, seq_len):
    cl = min(CHUNK, seq_len)
    reps = CHUNK // cl
    wm = (w_s * jnp.tril(jnp.ones((CHUNK, CHUNK), w_s.dtype)))[:, :cl, :cl]
    eye = jnp.eye(reps, dtype=w_s.dtype)
    wm = jnp.einsum("ab,gts->gatbs", eye, wm).reshape(A_GROUPS, CHUNK, CHUNK).astype(BF16)
    bias = jnp.tile(b_s[:, :cl].T, (reps, 1))
    bias = jnp.repeat(bias, A_GROUP_DIM, axis=1)
    return wm, bias


def _even_layer(x, batch, seq_len, offset, ew, w_out, ln_v_g, ln_v_b, w_s, b_s, g_q_a, g_kv_a, g_pre, g_post, past):
    w_in_t, w_gate, w_qa, w_kv, w_q, w_nope, w_vup = ew
    m = x.shape[0]
    h = rmsnorm_cast(x, g_pre)
    gates = mm_act(h, w_gate, col0=0, ncols=3 * W_A, n_gelu_cols=W_A, out_dtype=BF16,
                   transposed_w=True)
    gv = mm_act(h, w_in_t, col0=W_A, ncols=W_A, n_gelu_cols=W_A, out_dtype=F32, transposed_w=True)
    qn = mm_rms(h, w_qa, g_q_a)
    cs = _rope_cs(seq_len, offset, batch)
    c, k_pe, kcat = mm_kv(h, w_kv, g_kv_a, cs)

    wm, bias = _spatial_weights(w_s, b_s, seq_len)
    blocks_per_seq = max(seq_len // CHUNK, 1)
    out_a, v_open = _mix_a_call(gates, gv, ln_v_g, ln_v_b, wm, bias, blocks_per_seq)

    q = mm_q(qn, w_q, w_nope, cs)
    if past is None:
        out_b = mla_prefill(q, kcat.reshape(batch, seq_len, K_CAT), gates.reshape(batch, seq_len, 3 * W_A),
                            w_vup, zb_col_block=2).reshape(m, W_B)
    else:
        cache_c, cache_rt, layer, page_table = past
        per = QBLOCK // seq_len
        qd = q.reshape(m // QBLOCK, HEADS, per, seq_len, K_CAT)
        qd = jnp.transpose(qd, (0, 2, 1, 3, 4)).reshape(batch, HEADS * seq_len, K_CAT)
        knew = jnp.pad(kcat.reshape(batch, seq_len, K_CAT), ((0, 0), (0, PAGE - seq_len), (0, 0)))
        o_lat = mla_decode(qd, knew, cache_c, cache_rt, layer, page_table)
        o_heads = jnp.transpose(o_lat.reshape(batch, HEADS, seq_len, KV_RANK), (1, 0, 2, 3))
        out_b = v_up(o_heads.reshape(HEADS, m, KV_RANK), w_vup, gates, zb_col0=2 * W_A)
    y = mm_plain([out_a, out_b], w_out)
    return residual_norm(x, y, g_post), c, k_pe, v_open


def _mix_a_call(gates, gv, ln_v_g, ln_v_b, wm, bias, blocks_per_seq):
    m = gv.shape[0]
    nblk = m // CHUNK
    row0 = lambda i: (i, 0)
    row1 = lambda i: (i, 1)
    fixed2 = lambda i: (0, 0)
    return pl.pallas_call(
        _mix_a_kernel,
        grid=(nblk,),
        in_specs=[pl.BlockSpec((CHUNK, W_A), row0), pl.BlockSpec((CHUNK, W_A), row0),
                  pl.BlockSpec((CHUNK, W_A), row1),
                  pl.BlockSpec((1, W_A), fixed2), pl.BlockSpec((1, W_A), fixed2),
                  pl.BlockSpec((A_GROUPS, CHUNK, CHUNK), lambda i: (0, 0, 0)),
                  pl.BlockSpec((CHUNK, W_A), fixed2)],
        out_specs=[pl.BlockSpec((CHUNK, W_A), row0),
                   pl.BlockSpec((CHUNK, W_A), lambda i: (i // blocks_per_seq, 0))],
        out_shape=[jax.ShapeDtypeStruct((m, W_A), BF16),
                   jax.ShapeDtypeStruct((m // blocks_per_seq, W_A), F32)],
        compiler_params=_params_seq(("arbitrary",)),
        name="mix_a",
    )(gates, gv, gates, ln_v_g.reshape(1, W_A), ln_v_b.reshape(1, W_A), wm, bias)


def _odd_layer(x, batch, seq_len, conv_prev, w_in, w_dw, b_dw, ln_c_g, ln_c_b, w_out, g_pre, g_post):
    m, d = x.shape
    c = w_dw.shape[1]
    h = rmsnorm_cast(x, g_pre)
    g = mm_glu(h, w_in, c).reshape(batch, seq_len, c)
    sz = mm_act(h, w_in, col0=2 * c, ncols=c, n_gelu_cols=0, out_dtype=BF16).reshape(batch, seq_len, c)
    if conv_prev is None:
        y2 = conv_module(g, g, sz, w_dw, b_dw, ln_c_g, ln_c_b, tl=min(seq_len, 128), zero_first=True)
    else:
        y2 = conv_module(g, conv_prev, sz, w_dw, b_dw, ln_c_g, ln_c_b, tl=seq_len, zero_first=False)
    y = mm_plain([y2.reshape(m, c)], w_out)
    return residual_norm(x, y, g_post), g


def kernel(x_prompt, x_sample, cache_kv_latent, cache_k_rope, state_conv, page_table, w_in_even, ln_v_g, ln_v_b,
           w_s, b_s, g_q_a, w_q_b, g_kv_a, w_kv_b, w_out_even, w_in_odd, w_dw, b_dw, ln_c_g, ln_c_b, w_out_odd,
           g_pre, g_post):
    bp, lp, d = x_prompt.shape
    bs, ls, _ = x_sample.shape
    past_len = page_table.shape[1] * PAGE
    depth = g_pre.shape[0]
    xp = x_prompt.reshape(bp * lp, d)
    xs = x_sample.reshape(bs * ls, d)
    lat_p, kpe_p, v_p, conv_p = [], [], [], []
    lat_s, kpe_s, v_s, conv_s = [], [], [], []
    for i in range(depth):
        j = i // 2
        if i % 2 == 0:
            ew = _even_weights(w_in_even[j], w_q_b[j], w_kv_b[j])
            args = (ew, w_out_even[j], ln_v_g[j], ln_v_b[j], w_s[j], b_s[j], g_q_a[j], g_kv_a[j], g_pre[i], g_post[i])
            xp, c, k, v = _even_layer(xp, bp, lp, 0, *args, None)
            lat_p.append(c.reshape(bp, lp, KV_RANK))
            kpe_p.append(k.reshape(bp, lp, QK_ROPE))
            v_p.append(v.reshape(bp, -1, W_A))
            cache_rt = jnp.swapaxes(cache_k_rope, 2, 3)
            xs, c, k, v = _even_layer(xs, bs, ls, past_len, *args,
                                      (cache_kv_latent, cache_rt, j, page_table))
            lat_s.append(c.reshape(bs, ls, KV_RANK))
            kpe_s.append(k.reshape(bs, ls, QK_ROPE))
            v_s.append(v.reshape(bs, ls, W_A))
        else:
            args = (w_in_odd[j], w_dw[j], b_dw[j], ln_c_g[j], ln_c_b[j], w_out_odd[j], g_pre[i], g_post[i])
            xp, g = _odd_layer(xp, bp, lp, None, *args)
            conv_p.append(g[:, lp - (CONV_W - 1):])
            xs, g = _odd_layer(xs, bs, ls, state_conv[j], *args)
            conv_s.append(jnp.concatenate([state_conv[j], g], axis=1)[:, -(CONV_W - 1):])
    return (xp.reshape(bp, lp, d), xs.reshape(bs, ls, d), jnp.stack(lat_p), jnp.stack(kpe_p), jnp.stack(v_p),
            jnp.stack(conv_p), jnp.stack(lat_s), jnp.stack(kpe_s), jnp.stack(v_s), jnp.stack(conv_s))
```

```python
import functools
import math

import jax
import jax.numpy as jnp
from jax import lax
from jax.experimental import pallas as pl
from jax.experimental.pallas import tpu as pltpu

F32 = jnp.float32
BF16 = jnp.bfloat16

EPS = 1e-6
LANES = 128
SUBLANES = 8
CHUNK = 128
A_GROUPS = 16
A_GROUP_DIM = 128
W_A = A_GROUPS * A_GROUP_DIM
HEADS = 16
QK_NOPE = 128
QK_ROPE = 64
V_DIM = 128
Q_RANK = 1024
KV_RANK = 512
W_B = HEADS * V_DIM
K_CAT = KV_RANK + QK_ROPE
ROPE_THETA = 10000.0
QBLOCK = 128
PAGE = 128
DECODE_PAGES_PER_CHUNK = 16
DECODE_SLOTS = 3
CONV_W = 31
CONV_HALO = 32
CONV_TILE = 128
LN_ROWS = 32
ATTN_SCALE = 1.0 / math.sqrt(QK_NOPE + QK_ROPE)
NEG_INF = float("-inf")
VMEM_LIMIT_BYTES = 56 * 1024 * 1024


def _params(n_axes):
    return pltpu.CompilerParams(dimension_semantics=("parallel",) * n_axes,
                                vmem_limit_bytes=VMEM_LIMIT_BYTES)


def _params_seq(sem):
    return pltpu.CompilerParams(dimension_semantics=sem, vmem_limit_bytes=VMEM_LIMIT_BYTES)


def _sigmoid(x):
    return 1.0 / (1.0 + jnp.exp(-x))


def _silu(x):
    return x * _sigmoid(x)


def _rms(x, g):
    return x * lax.rsqrt(jnp.mean(x * x, axis=-1, keepdims=True) + EPS) * g


def _layer_norm(x, g, b):
    mu = jnp.mean(x, axis=-1, keepdims=True)
    xc = x - mu
    return xc * lax.rsqrt(jnp.mean(xc * xc, axis=-1, keepdims=True) + EPS) * g + b


def _tile(n, pref):
    return pref if n % pref == 0 else n


def _rmsnorm_kernel(x_ref, g_ref, o_ref):
    o_ref[...] = _rms(x_ref[...], g_ref[...]).astype(o_ref.dtype)


def rmsnorm_cast(x, g):
    m, d = x.shape
    tm = _tile(m, 256)
    return pl.pallas_call(
        _rmsnorm_kernel,
        grid=(m // tm,),
        in_specs=[pl.BlockSpec((tm, d), lambda i: (i, 0)), pl.BlockSpec((1, d), lambda i: (0, 0))],
        out_specs=pl.BlockSpec((tm, d), lambda i: (i, 0)),
        out_shape=jax.ShapeDtypeStruct((m, d), BF16),
        compiler_params=_params(1),
        name="rmsnorm_cast",
    )(x, g.reshape(1, d))


def _residual_norm_kernel(x_ref, y_ref, g_ref, o_ref):
    o_ref[...] = x_ref[...] + _rms(y_ref[...], g_ref[...])


def _residual_norm_next_kernel(x_ref, y_ref, g_ref, gn_ref, o_ref, h_ref):
    x_new = x_ref[...] + _rms(y_ref[...], g_ref[...])
    o_ref[...] = x_new
    h_ref[...] = _rms(x_new, gn_ref[...]).astype(h_ref.dtype)


def residual_norm(x, y, g, g_next=None):
    m, d = x.shape
    tm = _tile(m, 256)
    row = pl.BlockSpec((tm, d), lambda i: (i, 0))
    vec = pl.BlockSpec((1, d), lambda i: (0, 0))
    if g_next is None:
        return pl.pallas_call(
            _residual_norm_kernel,
            grid=(m // tm,),
            in_specs=[row, row, vec],
            out_specs=row,
            out_shape=jax.ShapeDtypeStruct((m, d), F32),
            compiler_params=_params(1),
            name="residual_norm",
        )(x, y, g.reshape(1, d)), None
    return pl.pallas_call(
        _residual_norm_next_kernel,
        grid=(m // tm,),
        in_specs=[row, row, vec, vec],
        out_specs=[row, row],
        out_shape=[jax.ShapeDtypeStruct((m, d), F32), jax.ShapeDtypeStruct((m, d), BF16)],
        compiler_params=_params(1),
        name="residual_norm_next",
    )(x, y, g.reshape(1, d), g_next.reshape(1, d))


def _bf16_dot(x, w, transposed_w=False):
    w = w.astype(BF16)
    if transposed_w:
        return _nt_dot(x, w)
    return jnp.dot(x, w, preferred_element_type=F32)


def _mm_act_kernel(x_ref, w_ref, o_ref, *, n_gelu_tiles, transposed_w):
    acc = _bf16_dot(x_ref[...], w_ref[...], transposed_w)
    j = pl.program_id(1)

    @pl.when(j < n_gelu_tiles)
    def _():
        o_ref[...] = jax.nn.gelu(acc).astype(o_ref.dtype)

    @pl.when(j >= n_gelu_tiles)
    def _():
        o_ref[...] = _silu(acc).astype(o_ref.dtype)


def mm_act(x, w, *, col0, ncols, n_gelu_cols, out_dtype, transposed_w=False):
    m, k = x.shape
    tm = _tile(m, 1024)
    tn = _tile(ncols, 512)
    assert col0 % tn == 0 and n_gelu_cols % tn == 0
    j0 = col0 // tn
    if transposed_w:
        w_spec = pl.BlockSpec((tn, k), lambda i, j: (j + j0, 0))
    else:
        w_spec = pl.BlockSpec((k, tn), lambda i, j: (0, j + j0))
    return pl.pallas_call(
        functools.partial(_mm_act_kernel, n_gelu_tiles=n_gelu_cols // tn, transposed_w=transposed_w),
        grid=(m // tm, ncols // tn),
        in_specs=[pl.BlockSpec((tm, k), lambda i, j: (i, 0)), w_spec],
        out_specs=pl.BlockSpec((tm, tn), lambda i, j: (i, j)),
        out_shape=jax.ShapeDtypeStruct((m, ncols), out_dtype),
        compiler_params=_params(2),
        name="mm_act",
    )(x, w)


def _mm_plain_kernel(*refs, n_in):
    x_refs = refs[:n_in]
    w_refs = refs[n_in:2 * n_in]
    o_ref = refs[2 * n_in]
    acc = _bf16_dot(x_refs[0][...], w_refs[0][...])
    for x_ref, w_ref in zip(x_refs[1:], w_refs[1:]):
        acc += _bf16_dot(x_ref[...], w_ref[...])
    o_ref[...] = acc


def mm_plain(xs, w):
    m = xs[0].shape[0]
    n = w.shape[1]
    tm = _tile(m, 1024)
    tn = _tile(n, 512)
    in_specs = [pl.BlockSpec((tm, x.shape[1]), lambda i, j: (i, 0)) for x in xs]
    row = 0
    for x in xs:
        kx = x.shape[1]
        assert row % kx == 0
        in_specs.append(pl.BlockSpec((kx, tn), functools.partial(lambda i, j, r: (r, j), r=row // kx)))
        row += kx
    assert row == w.shape[0]
    return pl.pallas_call(
        functools.partial(_mm_plain_kernel, n_in=len(xs)),
        grid=(m // tm, n // tn),
        in_specs=in_specs,
        out_specs=pl.BlockSpec((tm, tn), lambda i, j: (i, j)),
        out_shape=jax.ShapeDtypeStruct((m, n), F32),
        compiler_params=_params(2),
        name="mm_plain",
    )(*xs, *([w] * len(xs)))


def _mm_rms_kernel(x_ref, w_ref, g_ref, o_ref):
    acc = _nt_dot(x_ref[...], w_ref[...])
    o_ref[...] = _rms(acc, g_ref[...]).astype(o_ref.dtype)


def mm_rms(x, w_t, g):
    m, k = x.shape
    n = w_t.shape[0]
    tm = _tile(m, 1024)
    return pl.pallas_call(
        _mm_rms_kernel,
        grid=(m // tm,),
        in_specs=[pl.BlockSpec((tm, k), lambda i: (i, 0)), pl.BlockSpec((n, k), lambda i: (0, 0)),
                  pl.BlockSpec((1, n), lambda i: (0, 0))],
        out_specs=pl.BlockSpec((tm, n), lambda i: (i, 0)),
        out_shape=jax.ShapeDtypeStruct((m, n), BF16),
        compiler_params=_params(1),
        name="mm_rms",
    )(x, w_t, g.reshape(1, n))


def _rope_from_pair(pair, cs):
    t = pair * cs
    return (t + pltpu.roll(t, QK_ROPE, axis=1))[:, :QK_ROPE]


def _mm_kv_kernel(x_ref, w_ref, g_ref, cs_ref, c_ref, kpe_ref, kcat_ref):
    acc = _nt_dot(x_ref[...], w_ref[...])
    c = _rms(acc[:, :KV_RANK], g_ref[...])
    kpe = _rope_from_pair(acc[:, KV_RANK:], cs_ref[...])
    c_ref[...] = c
    kpe_ref[...] = kpe
    kcat_ref[:, :KV_RANK] = c.astype(BF16)
    kcat_ref[:, KV_RANK:] = kpe.astype(BF16)


def mm_kv(x, w_kv, g_kv, cs):
    m, k = x.shape
    n = w_kv.shape[0]
    tm = _tile(m, 1024)
    return pl.pallas_call(
        _mm_kv_kernel,
        grid=(m // tm,),
        in_specs=[pl.BlockSpec((tm, k), lambda i: (i, 0)), pl.BlockSpec((n, k), lambda i: (0, 0)),
                  pl.BlockSpec((1, KV_RANK), lambda i: (0, 0)), pl.BlockSpec((tm, LANES), lambda i: (i, 0))],
        out_specs=[pl.BlockSpec((tm, KV_RANK), lambda i: (i, 0)), pl.BlockSpec((tm, QK_ROPE), lambda i: (i, 0)),
                   pl.BlockSpec((tm, K_CAT), lambda i: (i, 0))],
        out_shape=[jax.ShapeDtypeStruct((m, KV_RANK), F32), jax.ShapeDtypeStruct((m, QK_ROPE), F32),
                   jax.ShapeDtypeStruct((m, K_CAT), BF16)],
        compiler_params=_params(1),
        name="mm_kv",
    )(x, w_kv, g_kv.reshape(1, KV_RANK), cs)


def _mm_q_kernel(x_ref, wq_ref, wn_ref, cs_ref, o_ref):
    x = x_ref[...]
    cs = cs_ref[...]
    groups = x.shape[0] // QBLOCK
    q_all = jnp.dot(x, wq_ref[...], preferred_element_type=F32)
    for h in range(HEADS):
        qh = q_all[:, h * 256:(h + 1) * 256]
        q_abs = jnp.dot(qh[:, :QK_NOPE].astype(BF16), wn_ref[h], preferred_element_type=F32) * ATTN_SCALE
        q_pe = _rope_from_pair(qh[:, QK_NOPE:], cs) * ATTN_SCALE
        o_ref[:, h, :, :KV_RANK] = q_abs.astype(BF16).reshape(groups, QBLOCK, KV_RANK)
        o_ref[:, h, :, KV_RANK:] = q_pe.astype(BF16).reshape(groups, QBLOCK, QK_ROPE)


def mm_q(qn, w_q, w_nope, cs):
    m, k = qn.shape
    tm = _tile(m, 512)
    return pl.pallas_call(
        _mm_q_kernel,
        grid=(m // tm,),
        in_specs=[pl.BlockSpec((tm, k), lambda i: (i, 0)),
                  pl.BlockSpec(w_q.shape, lambda i: (0, 0)),
                  pl.BlockSpec(w_nope.shape, lambda i: (0, 0, 0)),
                  pl.BlockSpec((tm, LANES), lambda i: (i, 0))],
        out_specs=pl.BlockSpec((tm // QBLOCK, HEADS, QBLOCK, K_CAT), lambda i: (i, 0, 0, 0)),
        out_shape=jax.ShapeDtypeStruct((m // QBLOCK, HEADS, QBLOCK, K_CAT), BF16),
        compiler_params=_params(1),
        name="mm_q",
    )(qn, w_q, w_nope, cs)


def _mm_glu_kernel(x_ref, wa_ref, wb_ref, o_ref):
    x = x_ref[...]
    a = _bf16_dot(x, wa_ref[...])
    b = _bf16_dot(x, wb_ref[...])
    o_ref[...] = a * _sigmoid(b)


def mm_glu(x, w, width):
    m, k = x.shape
    tm = _tile(m, 1024)
    tn = _tile(width, 256)
    nb = width // tn
    return pl.pallas_call(
        _mm_glu_kernel,
        grid=(m // tm, nb),
        in_specs=[pl.BlockSpec((tm, k), lambda i, j: (i, 0)),
                  pl.BlockSpec((k, tn), lambda i, j: (0, j)),
                  pl.BlockSpec((k, tn), lambda i, j: (0, j + nb))],
        out_specs=pl.BlockSpec((tm, tn), lambda i, j: (i, j)),
        out_shape=jax.ShapeDtypeStruct((m, width), F32),
        compiler_params=_params(2),
        name="mm_glu",
    )(x, w, w)


def _mix_a_kernel(gu_ref, gv_ref, sz_ref, lg_ref, lb_ref, wm_ref, bias_ref, oa_ref, vo_ref):
    vn = _layer_norm(gv_ref[...], lg_ref[...], lb_ref[...])
    vo_ref[...] = vn
    vb = vn.astype(BF16)
    for g in range(A_GROUPS):
        sl = slice(g * A_GROUP_DIM, (g + 1) * A_GROUP_DIM)
        mix = jnp.dot(wm_ref[g], vb[:, sl], preferred_element_type=F32) + bias_ref[:, sl]
        oa_ref[:, sl] = (gu_ref[:, sl].astype(F32) * mix * sz_ref[:, sl].astype(F32)).astype(BF16)


def _softmax_update(s, v, m_ref, l_ref, acc_ref):
    n_rep = s.shape[1] // LANES
    m_prev = m_ref[...]
    m_new = jnp.maximum(m_prev, jnp.max(s, axis=1, keepdims=True))
    alpha = jnp.exp(m_prev - m_new)
    p = jnp.exp(s - jnp.concatenate([m_new] * n_rep, axis=1))
    l_ref[...] = alpha * l_ref[...] + jnp.sum(p, axis=1, keepdims=True)
    m_ref[...] = m_new
    acc_ref[...] = (acc_ref[...] * jnp.concatenate([alpha] * (KV_RANK // LANES), axis=1)
                    + jnp.dot(p.astype(BF16), v, preferred_element_type=F32))


def _softmax_init(m_ref, l_ref, acc_ref):
    m_ref[...] = jnp.full(m_ref.shape, NEG_INF, F32)
    l_ref[...] = jnp.zeros(l_ref.shape, F32)
    acc_ref[...] = jnp.zeros(acc_ref.shape, F32)


def _softmax_result(l_ref, acc_ref):
    inv = 1.0 / l_ref[...]
    return acc_ref[...] * jnp.concatenate([inv] * (KV_RANK // LANES), axis=1)


def _nt_dot(a, b):
    return lax.dot_general(a, b, (((1,), (1,)), ((), ())), preferred_element_type=F32)


def _prefill_kernel(q_ref, k_ref, sz_ref, wv_ref, o_ref, m_ref, l_ref, acc_ref, *, kc):
    qb = pl.program_id(1)
    rows = HEADS * QBLOCK
    q = q_ref[0].reshape(rows, K_CAT)
    _softmax_init(m_ref, l_ref, acc_ref)

    def chunk(k0, masked):
        k = k_ref[0, pl.ds(k0, kc), :]
        s = _nt_dot(q, k)
        if masked:
            q_pos = qb * QBLOCK + (lax.broadcasted_iota(jnp.int32, (rows, kc), 0) & (QBLOCK - 1))
            k_pos = k0 + lax.broadcasted_iota(jnp.int32, (rows, kc), 1)
            s = jnp.where(k_pos <= q_pos, s, NEG_INF)
        _softmax_update(s, k[:, :KV_RANK], m_ref, l_ref, acc_ref)

    n_full = (qb * QBLOCK) // kc

    def body(i, carry):
        chunk(pl.multiple_of(i * kc, kc), False)
        return carry

    lax.fori_loop(0, n_full, body, 0)
    chunk(pl.multiple_of(n_full * kc, kc), True)

    o = _softmax_result(l_ref, acc_ref)
    for h in range(HEADS):
        sl = slice(h * V_DIM, (h + 1) * V_DIM)
        r = jnp.dot(o[h * QBLOCK:(h + 1) * QBLOCK].astype(BF16), wv_ref[h], preferred_element_type=F32)
        o_ref[0, :, sl] = (r * sz_ref[0, :, sl].astype(F32)).astype(BF16)


def mla_prefill(q, kcat, sz, w_v, zb_col_block):
    b, l, _ = kcat.shape
    nq = l // QBLOCK
    kc = _tile(l, 512)
    rows = HEADS * QBLOCK
    return pl.pallas_call(
        functools.partial(_prefill_kernel, kc=kc),
        grid=(b, nq),
        in_specs=[pl.BlockSpec((1, HEADS, QBLOCK, K_CAT), lambda i, j: (i * nq + j, 0, 0, 0)),
                  pl.BlockSpec((1, l, K_CAT), lambda i, j: (i, 0, 0)),
                  pl.BlockSpec((1, QBLOCK, W_B), lambda i, j: (i, j, zb_col_block)),
                  pl.BlockSpec(w_v.shape, lambda i, j: (0, 0, 0))],
        out_specs=pl.BlockSpec((1, QBLOCK, W_B), lambda i, j: (i, j, 0)),
        out_shape=jax.ShapeDtypeStruct((b, l, W_B), BF16),
        scratch_shapes=[pltpu.VMEM((rows, LANES), F32), pltpu.VMEM((rows, LANES), F32),
                        pltpu.VMEM((rows, KV_RANK), F32)],
        compiler_params=_params(2),
        name="mla_prefill",
    )(q, kcat, sz, w_v)


def _decode_kernel(pt_ref, q_ref, knew_ref, cc_hbm, cr_hbm, o_ref,
                   cbuf, rbuf, cb_ref, rb_ref, m_ref, l_ref, acc_ref, sem_c, sem_r,
                   *, layer, n_pages, ch, lq):
    b = pl.program_id(0)
    n_chunks = n_pages // ch
    total = pl.num_programs(0) * n_chunks
    ahead = DECODE_SLOTS - 1

    def chunk_copies(g):
        slot = lax.rem(g, DECODE_SLOTS)
        copies = []
        for p in range(ch):
            page = pt_ref[g * ch + p]
            copies.append(pltpu.make_async_copy(cc_hbm.at[layer, page], cbuf.at[slot, pl.ds(p * PAGE, PAGE)],
                                                sem_c.at[slot]))
            copies.append(pltpu.make_async_copy(cr_hbm.at[layer, page], rbuf.at[slot, p], sem_r.at[slot]))
        return copies

    def start_chunk(g):
        for cp in chunk_copies(g):
            cp.start()

    @pl.when(b == 0)
    def _():
        for g in range(ahead):
            start_chunk(g)

    _softmax_init(m_ref, l_ref, acc_ref)
    q = q_ref[0]

    def chunk_body(j, carry):
        g = b * n_chunks + j

        @pl.when(g + ahead < total)
        def _():
            start_chunk(g + ahead)

        for cp in chunk_copies(g):
            cp.wait()
        slot = lax.rem(g, DECODE_SLOTS)
        cb_ref[...] = cbuf[slot].astype(BF16)
        for p in range(ch):
            rb_ref[:, p * PAGE:(p + 1) * PAGE] = rbuf[slot, p].astype(BF16)
        cb = cb_ref[...]
        s = _nt_dot(q[:, :KV_RANK], cb) + jnp.dot(q[:, KV_RANK:], rb_ref[...], preferred_element_type=F32)
        _softmax_update(s, cb, m_ref, l_ref, acc_ref)
        return carry

    lax.fori_loop(0, n_chunks, chunk_body, 0)

    kn = knew_ref[0]
    rows = q.shape[0]
    s_new = _nt_dot(q, kn)
    q_pos = lax.broadcasted_iota(jnp.int32, (rows, PAGE), 0) & (lq - 1)
    k_pos = lax.broadcasted_iota(jnp.int32, (rows, PAGE), 1)
    s_new = jnp.where(k_pos <= q_pos, s_new, NEG_INF)
    _softmax_update(s_new, kn[:, :KV_RANK], m_ref, l_ref, acc_ref)
    o_ref[0] = _softmax_result(l_ref, acc_ref).astype(BF16)


def mla_decode(q, knew, cache_c, cache_rt, layer, page_table):
    bd, rows, _ = q.shape
    lq = rows // HEADS
    assert lq & (lq - 1) == 0
    n_pages = page_table.shape[1]
    ch = DECODE_PAGES_PER_CHUNK if n_pages % DECODE_PAGES_PER_CHUNK == 0 else n_pages
    assert bd * (n_pages // ch) >= DECODE_SLOTS
    seq_block = lambda b, pt: (b, 0, 0)
    grid_spec = pltpu.PrefetchScalarGridSpec(
        num_scalar_prefetch=1,
        grid=(bd,),
        in_specs=[pl.BlockSpec((1, rows, K_CAT), seq_block),
                  pl.BlockSpec((1, PAGE, K_CAT), seq_block),
                  pl.BlockSpec(memory_space=pl.ANY),
                  pl.BlockSpec(memory_space=pl.ANY)],
        out_specs=pl.BlockSpec((1, rows, KV_RANK), seq_block),
        scratch_shapes=[pltpu.VMEM((DECODE_SLOTS, ch * PAGE, KV_RANK), F32),
                        pltpu.VMEM((DECODE_SLOTS, ch, QK_ROPE, PAGE), F32),
                        pltpu.VMEM((ch * PAGE, KV_RANK), BF16), pltpu.VMEM((QK_ROPE, ch * PAGE), BF16),
                        pltpu.VMEM((rows, LANES), F32), pltpu.VMEM((rows, LANES), F32),
                        pltpu.VMEM((rows, KV_RANK), F32),
                        pltpu.SemaphoreType.DMA((DECODE_SLOTS,)), pltpu.SemaphoreType.DMA((DECODE_SLOTS,))],
    )
    return pl.pallas_call(
        functools.partial(_decode_kernel, layer=layer, n_pages=n_pages, ch=ch, lq=lq),
        grid_spec=grid_spec,
        out_shape=jax.ShapeDtypeStruct((bd, rows, KV_RANK), BF16),
        compiler_params=_params_seq(("arbitrary",)),
        name="mla_decode",
    )(page_table.reshape(-1), q, knew, cache_c, cache_rt)


def _v_up_kernel(o_ref, wv_ref, sz_ref, out_ref):
    r = jnp.dot(o_ref[0], wv_ref[0], preferred_element_type=F32)
    out_ref[...] = (r * sz_ref[...].astype(F32)).astype(BF16)


def v_up(o_heads, w_v, sz, zb_col0):
    _, m, _ = o_heads.shape
    jb = zb_col0 // V_DIM
    return pl.pallas_call(
        _v_up_kernel,
        grid=(HEADS,),
        in_specs=[pl.BlockSpec((1, m, KV_RANK), lambda h: (h, 0, 0)),
                  pl.BlockSpec((1, KV_RANK, V_DIM), lambda h: (h, 0, 0)),
                  pl.BlockSpec((m, V_DIM), lambda h: (0, h + jb))],
        out_specs=pl.BlockSpec((m, V_DIM), lambda h: (0, h)),
        out_shape=jax.ShapeDtypeStruct((m, W_B), BF16),
        compiler_params=_params(1),
        name="v_up",
    )(o_heads, w_v, sz)


def _conv_kernel(prev_ref, g_ref, sz_ref, w_ref, b_ref, lg_ref, lb_ref, o_ref, xp_ref, xs_ref, d_ref,
                 *, tl, lane_chunk):
    c = g_ref.shape[-1]
    xp_ref[:CONV_HALO, :] = jnp.where(pl.program_id(1) == 0, 0.0, prev_ref[0])
    xp_ref[CONV_HALO:CONV_HALO + tl, :] = g_ref[0]
    base = CONV_HALO - (CONV_W - 1)

    def lane_body(ci, carry):
        c0 = pl.multiple_of(ci * lane_chunk, lane_chunk)
        lanes = pl.ds(c0, lane_chunk)
        for phase in range(1, SUBLANES):
            xs_ref[phase] = xp_ref[phase:phase + tl + CONV_HALO - SUBLANES, lanes]
        acc = jnp.broadcast_to(b_ref[:, lanes], (tl, lane_chunk))
        for k in range(CONV_W):
            phase = (base + k) % SUBLANES
            off = (base + k) - phase
            if phase == 0:
                x = xp_ref[off:off + tl, lanes]
            else:
                x = xs_ref[phase, off:off + tl, :]
            acc = acc + w_ref[k:k + 1, lanes] * x
        d_ref[:, lanes] = acc
        return carry

    lax.fori_loop(0, c // lane_chunk, lane_body, 0)

    rows = min(tl, LN_ROWS)

    def norm_body(ri, carry):
        r = pl.ds(pl.multiple_of(ri * rows, rows), rows)
        y = _layer_norm(d_ref[r, :], lg_ref[...], lb_ref[...])
        o_ref[0, r, :] = (_silu(y) * sz_ref[0, r, :].astype(F32)).astype(BF16)
        return carry

    lax.fori_loop(0, tl // rows, norm_body, 0)


def conv_module(g, sz, w_dw, b_dw, ln_g, ln_b):
    b, l, c = g.shape
    tl = _tile(l, CONV_TILE)
    per = tl // CONV_HALO
    vec = lambda i, j: (0, 0)
    lane_chunk = LANES
    return pl.pallas_call(
        functools.partial(_conv_kernel, tl=tl, lane_chunk=lane_chunk),
        grid=(b, l // tl),
        in_specs=[pl.BlockSpec((1, CONV_HALO, c), lambda i, j: (i, jnp.maximum(j * per - 1, 0), 0)),
                  pl.BlockSpec((1, tl, c), lambda i, j: (i, j, 0)),
                  pl.BlockSpec((1, tl, c), lambda i, j: (i, j, 0)),
                  pl.BlockSpec((CONV_W, c), vec), pl.BlockSpec((1, c), vec),
                  pl.BlockSpec((1, c), vec), pl.BlockSpec((1, c), vec)],
        out_specs=pl.BlockSpec((1, tl, c), lambda i, j: (i, j, 0)),
        out_shape=jax.ShapeDtypeStruct((b, l, c), BF16),
        scratch_shapes=[pltpu.VMEM((CONV_HALO + tl, c), F32),
                        pltpu.VMEM((SUBLANES, tl + CONV_HALO - SUBLANES, lane_chunk), F32),
                        pltpu.VMEM((tl, c), F32)],
        compiler_params=_params(2),
        name="conv_module",
    )(g, g, sz, w_dw, b_dw.reshape(1, c), ln_g.reshape(1, c), ln_b.reshape(1, c))


def _conv_step_kernel(st_ref, g_ref, sz_ref, w_ref, b_ref, lg_ref, lb_ref, o_ref, ns_ref, d_ref, *, lane_chunk):
    n_prev = st_ref.shape[0]
    lq, bb, c = g_ref.shape

    def row(j, lanes):
        return st_ref[j, :, lanes] if j < n_prev else g_ref[j - n_prev, :, lanes]

    def lane_body(ci, carry):
        lanes = pl.ds(pl.multiple_of(ci * lane_chunk, lane_chunk), lane_chunk)
        acc = [jnp.broadcast_to(b_ref[:, lanes], (bb, lane_chunk)) for _ in range(lq)]
        for j in range(n_prev - (CONV_W - 1), n_prev + lq):
            x = row(j, lanes)
            for t in range(lq):
                k = j - (n_prev - (CONV_W - 1)) - t
                if 0 <= k < CONV_W:
                    acc[t] = acc[t] + w_ref[k:k + 1, lanes] * x
        for t in range(lq):
            d_ref[t, :, lanes] = acc[t]
        return carry

    lax.fori_loop(0, c // lane_chunk, lane_body, 0)
    for t in range(lq):
        y = _layer_norm(d_ref[t], lg_ref[...], lb_ref[...])
        o_ref[t] = (_silu(y) * sz_ref[t].astype(F32)).astype(BF16)
    ns_ref[:n_prev - lq] = st_ref[lq:]
    ns_ref[n_prev - lq:] = g_ref[...]


def conv_step(state_t, g_t, sz_t, w_dw, b_dw, ln_g, ln_b):
    n_prev, b, c = state_t.shape
    lq = g_t.shape[0]
    assert n_prev == CONV_W - 1 and lq < n_prev
    bb = _tile(b, 16)
    vec = lambda i: (0, 0)
    blk = lambda i: (0, i, 0)
    return pl.pallas_call(
        functools.partial(_conv_step_kernel, lane_chunk=min(c, 4 * LANES)),
        grid=(b // bb,),
        in_specs=[pl.BlockSpec((n_prev, bb, c), blk), pl.BlockSpec((lq, bb, c), blk), pl.BlockSpec((lq, bb, c), blk),
                  pl.BlockSpec((CONV_W, c), vec), pl.BlockSpec((1, c), vec),
                  pl.BlockSpec((1, c), vec), pl.BlockSpec((1, c), vec)],
        out_specs=[pl.BlockSpec((lq, bb, c), blk), pl.BlockSpec((n_prev, bb, c), blk)],
        out_shape=[jax.ShapeDtypeStruct((lq, b, c), BF16), jax.ShapeDtypeStruct((n_prev, b, c), F32)],
        scratch_shapes=[pltpu.VMEM((lq, bb, c), F32)],
        compiler_params=_params(1),
        name="conv_step",
    )(state_t, g_t, sz_t, w_dw, b_dw.reshape(1, c), ln_g.reshape(1, c), ln_b.reshape(1, c))


def _rope_cs(length, offset, reps):
    inv = jnp.power(ROPE_THETA, -jnp.arange(0, QK_ROPE, 2, dtype=F32) / QK_ROPE)
    pos = jnp.arange(length, dtype=F32) + offset
    ang = pos[:, None] * inv[None, :]
    cos, sin = jnp.cos(ang), jnp.sin(ang)
    return jnp.tile(jnp.concatenate([cos, cos, -sin, sin], axis=-1), (reps, 1))


def _swap_halves(w):
    half = w.shape[-1] // 2
    return jnp.concatenate([w[..., half:], w[..., :half]], axis=-1)


def _even_weights(w_in, w_q_b, w_kv_b):
    wt = w_in.T
    o_v, o_za = W_A, 2 * W_A
    o_qa = o_za + W_A
    o_kv = o_qa + Q_RANK
    o_kr = o_kv + KV_RANK
    o_zb = o_kr + QK_ROPE
    w_kr = wt[o_kr:o_zb]
    w_kr_sw = jnp.concatenate([w_kr[QK_ROPE // 2:], w_kr[:QK_ROPE // 2]], axis=0)
    w_gate = jnp.concatenate([wt[:o_v], wt[o_za:o_qa], wt[o_zb:]], axis=0).astype(BF16)
    w_qa = wt[o_qa:o_kv].astype(BF16)
    w_kv = jnp.concatenate([wt[o_kv:o_kr], w_kr, w_kr_sw], axis=0).astype(BF16)
    wq = w_q_b.reshape(Q_RANK, HEADS, QK_NOPE + QK_ROPE)
    wq_rope = wq[..., QK_NOPE:]
    w_q = jnp.concatenate([wq[..., :QK_NOPE], wq_rope, _swap_halves(wq_rope)], axis=-1)
    w_q = w_q.reshape(Q_RANK, HEADS * 256).astype(BF16)
    wkvb = w_kv_b.reshape(KV_RANK, HEADS, QK_NOPE + V_DIM)
    w_nope = jnp.transpose(wkvb[..., :QK_NOPE], (1, 2, 0)).astype(BF16)
    w_vup = jnp.transpose(wkvb[..., QK_NOPE:], (1, 0, 2)).astype(BF16)
    return wt, w_gate, w_qa, w_kv, w_q, w_nope, w_vup


def _spatial_weights(w_s, b_s, seq_len):
    cl = min(CHUNK, seq_len)
    reps = CHUNK // cl
    wm = (w_s * jnp.tril(jnp.ones((CHUNK, CHUNK), w_s.dtype)))[:, :cl, :cl]
    eye = jnp.eye(reps, dtype=w_s.dtype)
    wm = jnp.einsum("ab,gts->gatbs", eye, wm).reshape(A_GROUPS, CHUNK, CHUNK).astype(BF16)
    bias = jnp.tile(b_s[:, :cl].T, (reps, 1))
    bias = jnp.repeat(bias, A_GROUP_DIM, axis=1)
    return wm, bias


def _even_layer(x, h, batch, seq_len, offset, ew, w_out, ln_v_g, ln_v_b, w_s, b_s, g_q_a, g_kv_a, g_post, g_next,
                past):
    w_in_t, w_gate, w_qa, w_kv, w_q, w_nope, w_vup = ew
    m = x.shape[0]
    gates = mm_act(h, w_gate, col0=0, ncols=3 * W_A, n_gelu_cols=W_A, out_dtype=BF16,
                   transposed_w=True)
    gv = mm_act(h, w_in_t, col0=W_A, ncols=W_A, n_gelu_cols=W_A, out_dtype=F32, transposed_w=True)
    qn = mm_rms(h, w_qa, g_q_a)
    cs = _rope_cs(seq_len, offset, batch)
    c, k_pe, kcat = mm_kv(h, w_kv, g_kv_a, cs)

    wm, bias = _spatial_weights(w_s, b_s, seq_len)
    blocks_per_seq = max(seq_len // CHUNK, 1)
    out_a, v_open = _mix_a_call(gates, gv, ln_v_g, ln_v_b, wm, bias, blocks_per_seq)

    q = mm_q(qn, w_q, w_nope, cs)
    if past is None:
        out_b = mla_prefill(q, kcat.reshape(batch, seq_len, K_CAT), gates.reshape(batch, seq_len, 3 * W_A),
                            w_vup, zb_col_block=2).reshape(m, W_B)
    else:
        cache_c, cache_rt, layer, page_table = past
        per = QBLOCK // seq_len
        qd = q.reshape(m // QBLOCK, HEADS, per, seq_len, K_CAT)
        qd = jnp.transpose(qd, (0, 2, 1, 3, 4)).reshape(batch, HEADS * seq_len, K_CAT)
        knew = jnp.pad(kcat.reshape(batch, seq_len, K_CAT), ((0, 0), (0, PAGE - seq_len), (0, 0)))
        o_lat = mla_decode(qd, knew, cache_c, cache_rt, layer, page_table)
        o_heads = jnp.transpose(o_lat.reshape(batch, HEADS, seq_len, KV_RANK), (1, 0, 2, 3))
        out_b = v_up(o_heads.reshape(HEADS, m, KV_RANK), w_vup, gates, zb_col0=2 * W_A)
    y = mm_plain([out_a, out_b], w_out)
    x_new, h_next = residual_norm(x, y, g_post, g_next)
    return x_new, h_next, c, k_pe, v_open


def _mix_a_call(gates, gv, ln_v_g, ln_v_b, wm, bias, blocks_per_seq):
    m = gv.shape[0]
    nblk = m // CHUNK
    row0 = lambda i: (i, 0)
    row1 = lambda i: (i, 1)
    fixed2 = lambda i: (0, 0)
    return pl.pallas_call(
        _mix_a_kernel,
        grid=(nblk,),
        in_specs=[pl.BlockSpec((CHUNK, W_A), row0), pl.BlockSpec((CHUNK, W_A), row0),
                  pl.BlockSpec((CHUNK, W_A), row1),
                  pl.BlockSpec((1, W_A), fixed2), pl.BlockSpec((1, W_A), fixed2),
                  pl.BlockSpec((A_GROUPS, CHUNK, CHUNK), lambda i: (0, 0, 0)),
                  pl.BlockSpec((CHUNK, W_A), fixed2)],
        out_specs=[pl.BlockSpec((CHUNK, W_A), row0),
                   pl.BlockSpec((CHUNK, W_A), lambda i: (i // blocks_per_seq, 0))],
        out_shape=[jax.ShapeDtypeStruct((m, W_A), BF16),
                   jax.ShapeDtypeStruct((m // blocks_per_seq, W_A), F32)],
        compiler_params=_params_seq(("arbitrary",)),
        name="mix_a",
    )(gates, gv, gates, ln_v_g.reshape(1, W_A), ln_v_b.reshape(1, W_A), wm, bias)


def _odd_layer(x, h, batch, seq_len, conv_prev, w_in, w_dw, b_dw, ln_c_g, ln_c_b, w_out, g_post, g_next):
    m = x.shape[0]
    c = w_dw.shape[1]
    g = mm_glu(h, w_in, c).reshape(batch, seq_len, c)
    sz = mm_act(h, w_in, col0=2 * c, ncols=c, n_gelu_cols=0, out_dtype=BF16).reshape(batch, seq_len, c)
    if conv_prev is None:
        y2 = conv_module(g, sz, w_dw, b_dw, ln_c_g, ln_c_b)
        new_state = g[:, seq_len - (CONV_W - 1):]
    else:
        tm = lambda a: jnp.swapaxes(a, 0, 1)
        y2, new_state = conv_step(tm(conv_prev), tm(g), tm(sz), w_dw, b_dw, ln_c_g, ln_c_b)
        y2, new_state = tm(y2), tm(new_state)
    y = mm_plain([y2.reshape(m, c)], w_out)
    x_new, h_next = residual_norm(x, y, g_post, g_next)
    return x_new, h_next, new_state


def kernel(x_prompt, x_sample, cache_kv_latent, cache_k_rope, state_conv, page_table, w_in_even, ln_v_g, ln_v_b,
           w_s, b_s, g_q_a, w_q_b, g_kv_a, w_kv_b, w_out_even, w_in_odd, w_dw, b_dw, ln_c_g, ln_c_b, w_out_odd,
           g_pre, g_post):
    bp, lp, d = x_prompt.shape
    bs, ls, _ = x_sample.shape
    past_len = page_table.shape[1] * PAGE
    depth = g_pre.shape[0]
    xp = x_prompt.reshape(bp * lp, d)
    xs = x_sample.reshape(bs * ls, d)
    lat_p, kpe_p, v_p, conv_p = [], [], [], []
    lat_s, kpe_s, v_s, conv_s = [], [], [], []
    hp = rmsnorm_cast(xp, g_pre[0])
    hs = rmsnorm_cast(xs, g_pre[0])
    for i in range(depth):
        j = i // 2
        g_next = g_pre[i + 1] if i + 1 < depth else None
        if i % 2 == 0:
            ew = _even_weights(w_in_even[j], w_q_b[j], w_kv_b[j])
            args = (ew, w_out_even[j], ln_v_g[j], ln_v_b[j], w_s[j], b_s[j], g_q_a[j], g_kv_a[j], g_post[i], g_next)
            xp, hp, c, k, v = _even_layer(xp, hp, bp, lp, 0, *args, None)
            lat_p.append(c.reshape(bp, lp, KV_RANK))
            kpe_p.append(k.reshape(bp, lp, QK_ROPE))
            v_p.append(v.reshape(bp, -1, W_A))
            cache_rt = jnp.swapaxes(cache_k_rope, 2, 3)
            xs, hs, c, k, v = _even_layer(xs, hs, bs, ls, past_len, *args,
                                          (cache_kv_latent, cache_rt, j, page_table))
            lat_s.append(c.reshape(bs, ls, KV_RANK))
            kpe_s.append(k.reshape(bs, ls, QK_ROPE))
            v_s.append(v.reshape(bs, ls, W_A))
        else:
            args = (w_in_odd[j], w_dw[j], b_dw[j], ln_c_g[j], ln_c_b[j], w_out_odd[j], g_post[i], g_next)
            xp, hp, new_state = _odd_layer(xp, hp, bp, lp, None, *args)
            conv_p.append(new_state)
            xs, hs, new_state = _odd_layer(xs, hs, bs, ls, state_conv[j], *args)
            conv_s.append(new_state)
    return (xp.reshape(bp, lp, d), xs.reshape(bs, ls, d), jnp.stack(lat_p), jnp.stack(kpe_p), jnp.stack(v_p),
            jnp.stack(conv_p), jnp.stack(lat_s), jnp.stack(kpe_s), jnp.stack(v_s), jnp.stack(conv_s))
```

```python
import functools
import math

import jax
import jax.numpy as jnp
from jax import lax
from jax.experimental import pallas as pl
from jax.experimental.pallas import tpu as pltpu

F32 = jnp.float32
BF16 = jnp.bfloat16

EPS = 1e-6
LANES = 128
SUBLANES = 8
CHUNK = 128
A_GROUPS = 16
A_GROUP_DIM = 128
W_A = A_GROUPS * A_GROUP_DIM
HEADS = 16
QK_NOPE = 128
QK_ROPE = 64
V_DIM = 128
Q_RANK = 1024
KV_RANK = 512
W_B = HEADS * V_DIM
K_CAT = KV_RANK + QK_ROPE
ROPE_THETA = 10000.0
QBLOCK = 128
PAGE = 128
DECODE_PAGES_PER_CHUNK = 16
DECODE_SLOTS = 3
CONV_W = 31
CONV_HALO = 32
CONV_TILE = 128
LN_ROWS = 32
ATTN_SCALE = 1.0 / math.sqrt(QK_NOPE + QK_ROPE)
NEG_INF = float("-inf")
VMEM_LIMIT_BYTES = 56 * 1024 * 1024


def _params(n_axes):
    return pltpu.CompilerParams(dimension_semantics=("parallel",) * n_axes,
                                vmem_limit_bytes=VMEM_LIMIT_BYTES)


def _params_seq(sem):
    return pltpu.CompilerParams(dimension_semantics=sem, vmem_limit_bytes=VMEM_LIMIT_BYTES)


def _sigmoid(x):
    return 1.0 / (1.0 + jnp.exp(-x))


def _silu(x):
    return x * _sigmoid(x)


def _rms(x, g):
    return x * lax.rsqrt(jnp.mean(x * x, axis=-1, keepdims=True) + EPS) * g


def _layer_norm(x, g, b):
    mu = jnp.mean(x, axis=-1, keepdims=True)
    xc = x - mu
    return xc * lax.rsqrt(jnp.mean(xc * xc, axis=-1, keepdims=True) + EPS) * g + b


def _tile(n, pref):
    return pref if n % pref == 0 else n


def _rmsnorm_kernel(x_ref, g_ref, o_ref):
    o_ref[...] = _rms(x_ref[...], g_ref[...]).astype(o_ref.dtype)


def rmsnorm_cast(x, g):
    m, d = x.shape
    tm = _tile(m, 256)
    return pl.pallas_call(
        _rmsnorm_kernel,
        grid=(m // tm,),
        in_specs=[pl.BlockSpec((tm, d), lambda i: (i, 0)), pl.BlockSpec((1, d), lambda i: (0, 0))],
        out_specs=pl.BlockSpec((tm, d), lambda i: (i, 0)),
        out_shape=jax.ShapeDtypeStruct((m, d), BF16),
        compiler_params=_params(1),
        name="rmsnorm_cast",
    )(x, g.reshape(1, d))


def _residual_norm_kernel(x_ref, y_ref, g_ref, o_ref):
    o_ref[...] = x_ref[...] + _rms(y_ref[...].astype(F32), g_ref[...])


def _residual_norm_next_kernel(x_ref, y_ref, g_ref, gn_ref, o_ref, h_ref):
    x_new = x_ref[...] + _rms(y_ref[...].astype(F32), g_ref[...])
    o_ref[...] = x_new
    h_ref[...] = _rms(x_new, gn_ref[...]).astype(h_ref.dtype)


def residual_norm(x, y, g, g_next=None):
    m, d = x.shape
    tm = _tile(m, 256)
    row = pl.BlockSpec((tm, d), lambda i: (i, 0))
    vec = pl.BlockSpec((1, d), lambda i: (0, 0))
    if g_next is None:
        return pl.pallas_call(
            _residual_norm_kernel,
            grid=(m // tm,),
            in_specs=[row, row, vec],
            out_specs=row,
            out_shape=jax.ShapeDtypeStruct((m, d), F32),
            compiler_params=_params(1),
            name="residual_norm",
        )(x, y, g.reshape(1, d)), None
    return pl.pallas_call(
        _residual_norm_next_kernel,
        grid=(m // tm,),
        in_specs=[row, row, vec, vec],
        out_specs=[row, row],
        out_shape=[jax.ShapeDtypeStruct((m, d), F32), jax.ShapeDtypeStruct((m, d), BF16)],
        compiler_params=_params(1),
        name="residual_norm_next",
    )(x, y, g.reshape(1, d), g_next.reshape(1, d))


_WEIGHT_STATIONARY = ("arbitrary", "arbitrary")


def _stage_weight(w_ref, wb_ref):
    @pl.when(pl.program_id(1) == 0)
    def _():
        wb_ref[...] = w_ref[...].astype(BF16)


def _mm_act_kernel(x_ref, w_ref, o_ref, wb_ref, *, n_gelu_tiles, transposed_w):
    _stage_weight(w_ref, wb_ref)
    if transposed_w:
        acc = _nt_dot(x_ref[...], wb_ref[...])
    else:
        acc = jnp.dot(x_ref[...], wb_ref[...], preferred_element_type=F32)
    j = pl.program_id(0)

    @pl.when(j < n_gelu_tiles)
    def _():
        o_ref[...] = jax.nn.gelu(acc).astype(o_ref.dtype)

    @pl.when(j >= n_gelu_tiles)
    def _():
        o_ref[...] = _silu(acc).astype(o_ref.dtype)


def mm_act(x, w, *, col0, ncols, n_gelu_cols, out_dtype, transposed_w=False):
    m, k = x.shape
    tm = _tile(m, 1024)
    tn = _tile(ncols, 512)
    assert col0 % tn == 0 and n_gelu_cols % tn == 0
    j0 = col0 // tn
    if transposed_w:
        w_block, w_map = (tn, k), lambda j, i: (j + j0, 0)
    else:
        w_block, w_map = (k, tn), lambda j, i: (0, j + j0)
    return pl.pallas_call(
        functools.partial(_mm_act_kernel, n_gelu_tiles=n_gelu_cols // tn, transposed_w=transposed_w),
        grid=(ncols // tn, m // tm),
        in_specs=[pl.BlockSpec((tm, k), lambda j, i: (i, 0)), pl.BlockSpec(w_block, w_map)],
        out_specs=pl.BlockSpec((tm, tn), lambda j, i: (i, j)),
        out_shape=jax.ShapeDtypeStruct((m, ncols), out_dtype),
        scratch_shapes=[pltpu.VMEM(w_block, BF16)],
        compiler_params=_params_seq(_WEIGHT_STATIONARY),
        name="mm_act",
    )(x, w)


def _mm_plain_kernel(*refs, n_in):
    x_refs = refs[:n_in]
    w_refs = refs[n_in:2 * n_in]
    o_ref = refs[2 * n_in]
    wb_refs = refs[2 * n_in + 1:]
    acc = None
    for x_ref, w_ref, wb_ref in zip(x_refs, w_refs, wb_refs):
        _stage_weight(w_ref, wb_ref)
        part = jnp.dot(x_ref[...], wb_ref[...], preferred_element_type=F32)
        acc = part if acc is None else acc + part
    o_ref[...] = acc.astype(o_ref.dtype)


def mm_plain(xs, w, out_dtype):
    m = xs[0].shape[0]
    n = w.shape[1]
    tm = _tile(m, 1024)
    tn = _tile(n, 512)
    in_specs = [pl.BlockSpec((tm, x.shape[1]), lambda j, i: (i, 0)) for x in xs]
    row = 0
    for x in xs:
        kx = x.shape[1]
        assert row % kx == 0
        in_specs.append(pl.BlockSpec((kx, tn), functools.partial(lambda j, i, r: (r, j), r=row // kx)))
        row += kx
    assert row == w.shape[0]
    return pl.pallas_call(
        functools.partial(_mm_plain_kernel, n_in=len(xs)),
        grid=(n // tn, m // tm),
        in_specs=in_specs,
        out_specs=pl.BlockSpec((tm, tn), lambda j, i: (i, j)),
        out_shape=jax.ShapeDtypeStruct((m, n), out_dtype),
        scratch_shapes=[pltpu.VMEM((x.shape[1], tn), BF16) for x in xs],
        compiler_params=_params_seq(_WEIGHT_STATIONARY),
        name="mm_plain",
    )(*xs, *([w] * len(xs)))


def _mm_rms_kernel(x_ref, w_ref, g_ref, o_ref):
    acc = _nt_dot(x_ref[...], w_ref[...])
    o_ref[...] = _rms(acc, g_ref[...]).astype(o_ref.dtype)


def mm_rms(x, w_t, g):
    m, k = x.shape
    n = w_t.shape[0]
    tm = _tile(m, 1024)
    return pl.pallas_call(
        _mm_rms_kernel,
        grid=(m // tm,),
        in_specs=[pl.BlockSpec((tm, k), lambda i: (i, 0)), pl.BlockSpec((n, k), lambda i: (0, 0)),
                  pl.BlockSpec((1, n), lambda i: (0, 0))],
        out_specs=pl.BlockSpec((tm, n), lambda i: (i, 0)),
        out_shape=jax.ShapeDtypeStruct((m, n), BF16),
        compiler_params=_params(1),
        name="mm_rms",
    )(x, w_t, g.reshape(1, n))


def _rope_from_pair(pair, cs):
    t = pair * cs
    return (t + pltpu.roll(t, QK_ROPE, axis=1))[:, :QK_ROPE]


def _mm_kv_kernel(x_ref, w_ref, g_ref, cs_ref, c_ref, kpe_ref, kcat_ref):
    acc = _nt_dot(x_ref[...], w_ref[...])
    c = _rms(acc[:, :KV_RANK], g_ref[...])
    kpe = _rope_from_pair(acc[:, KV_RANK:], cs_ref[...])
    c_ref[...] = c
    kpe_ref[...] = kpe
    kcat_ref[:, :KV_RANK] = c.astype(BF16)
    kcat_ref[:, KV_RANK:] = kpe.astype(BF16)


def mm_kv(x, w_kv, g_kv, cs):
    m, k = x.shape
    n = w_kv.shape[0]
    tm = _tile(m, 1024)
    return pl.pallas_call(
        _mm_kv_kernel,
        grid=(m // tm,),
        in_specs=[pl.BlockSpec((tm, k), lambda i: (i, 0)), pl.BlockSpec((n, k), lambda i: (0, 0)),
                  pl.BlockSpec((1, KV_RANK), lambda i: (0, 0)), pl.BlockSpec((tm, LANES), lambda i: (i, 0))],
        out_specs=[pl.BlockSpec((tm, KV_RANK), lambda i: (i, 0)), pl.BlockSpec((tm, QK_ROPE), lambda i: (i, 0)),
                   pl.BlockSpec((tm, K_CAT), lambda i: (i, 0))],
        out_shape=[jax.ShapeDtypeStruct((m, KV_RANK), F32), jax.ShapeDtypeStruct((m, QK_ROPE), F32),
                   jax.ShapeDtypeStruct((m, K_CAT), BF16)],
        compiler_params=_params(1),
        name="mm_kv",
    )(x, w_kv, g_kv.reshape(1, KV_RANK), cs)


def _mm_q_kernel(x_ref, wq_ref, wn_ref, cs_ref, o_ref):
    x = x_ref[...]
    cs = cs_ref[...]
    groups = x.shape[0] // QBLOCK
    q_all = jnp.dot(x, wq_ref[...], preferred_element_type=F32)
    for h in range(HEADS):
        qh = q_all[:, h * 256:(h + 1) * 256]
        q_abs = jnp.dot(qh[:, :QK_NOPE].astype(BF16), wn_ref[h], preferred_element_type=F32) * ATTN_SCALE
        q_pe = _rope_from_pair(qh[:, QK_NOPE:], cs) * ATTN_SCALE
        o_ref[:, h, :, :KV_RANK] = q_abs.astype(BF16).reshape(groups, QBLOCK, KV_RANK)
        o_ref[:, h, :, KV_RANK:] = q_pe.astype(BF16).reshape(groups, QBLOCK, QK_ROPE)


def mm_q(qn, w_q, w_nope, cs):
    m, k = qn.shape
    tm = _tile(m, 512)
    return pl.pallas_call(
        _mm_q_kernel,
        grid=(m // tm,),
        in_specs=[pl.BlockSpec((tm, k), lambda i: (i, 0)),
                  pl.BlockSpec(w_q.shape, lambda i: (0, 0)),
                  pl.BlockSpec(w_nope.shape, lambda i: (0, 0, 0)),
                  pl.BlockSpec((tm, LANES), lambda i: (i, 0))],
        out_specs=pl.BlockSpec((tm // QBLOCK, HEADS, QBLOCK, K_CAT), lambda i: (i, 0, 0, 0)),
        out_shape=jax.ShapeDtypeStruct((m // QBLOCK, HEADS, QBLOCK, K_CAT), BF16),
        compiler_params=_params(1),
        name="mm_q",
    )(qn, w_q, w_nope, cs)


def _mm_glu_kernel(x_ref, wa_ref, wg_ref, o_ref, wab_ref, wgb_ref):
    _stage_weight(wa_ref, wab_ref)
    _stage_weight(wg_ref, wgb_ref)
    x = x_ref[...]
    a = jnp.dot(x, wab_ref[...], preferred_element_type=F32)
    b = jnp.dot(x, wgb_ref[...], preferred_element_type=F32)
    o_ref[...] = a * _sigmoid(b)


def mm_glu(x, w, width):
    m, k = x.shape
    tm = _tile(m, 1024)
    tn = _tile(width, 256)
    nb = width // tn
    return pl.pallas_call(
        _mm_glu_kernel,
        grid=(nb, m // tm),
        in_specs=[pl.BlockSpec((tm, k), lambda j, i: (i, 0)),
                  pl.BlockSpec((k, tn), lambda j, i: (0, j)),
                  pl.BlockSpec((k, tn), lambda j, i: (0, j + nb))],
        out_specs=pl.BlockSpec((tm, tn), lambda j, i: (i, j)),
        out_shape=jax.ShapeDtypeStruct((m, width), F32),
        scratch_shapes=[pltpu.VMEM((k, tn), BF16), pltpu.VMEM((k, tn), BF16)],
        compiler_params=_params_seq(_WEIGHT_STATIONARY),
        name="mm_glu",
    )(x, w, w)


def _mix_a_kernel(gu_ref, gv_ref, sz_ref, lg_ref, lb_ref, wm_ref, bias_ref, oa_ref, vo_ref):
    vn = _layer_norm(gv_ref[...], lg_ref[...], lb_ref[...])
    vo_ref[...] = vn
    vb = vn.astype(BF16)
    for g in range(A_GROUPS):
        sl = slice(g * A_GROUP_DIM, (g + 1) * A_GROUP_DIM)
        mix = jnp.dot(wm_ref[g], vb[:, sl], preferred_element_type=F32) + bias_ref[:, sl]
        oa_ref[:, sl] = (gu_ref[:, sl].astype(F32) * mix * sz_ref[:, sl].astype(F32)).astype(BF16)


def _softmax_update(s, v, m_ref, l_ref, acc_ref):
    n_rep = s.shape[1] // LANES
    m_prev = m_ref[...]
    m_new = jnp.maximum(m_prev, jnp.max(s, axis=1, keepdims=True))
    alpha = jnp.exp(m_prev - m_new)
    p = jnp.exp(s - jnp.concatenate([m_new] * n_rep, axis=1))
    l_ref[...] = alpha * l_ref[...] + jnp.sum(p, axis=1, keepdims=True)
    m_ref[...] = m_new
    acc_ref[...] = (acc_ref[...] * jnp.concatenate([alpha] * (KV_RANK // LANES), axis=1)
                    + jnp.dot(p.astype(BF16), v, preferred_element_type=F32))


def _softmax_init(m_ref, l_ref, acc_ref):
    m_ref[...] = jnp.full(m_ref.shape, NEG_INF, F32)
    l_ref[...] = jnp.zeros(l_ref.shape, F32)
    acc_ref[...] = jnp.zeros(acc_ref.shape, F32)


def _softmax_result(l_ref, acc_ref):
    inv = 1.0 / l_ref[...]
    return acc_ref[...] * jnp.concatenate([inv] * (KV_RANK // LANES), axis=1)


def _nt_dot(a, b):
    return lax.dot_general(a, b, (((1,), (1,)), ((), ())), preferred_element_type=F32)


def _prefill_kernel(q_ref, k_ref, sz_ref, wv_ref, o_ref, m_ref, l_ref, acc_ref, *, kc):
    qb = pl.program_id(1)
    rows = HEADS * QBLOCK
    q = q_ref[0].reshape(rows, K_CAT)
    _softmax_init(m_ref, l_ref, acc_ref)

    def chunk(k0, width, masked):
        k = k_ref[0, pl.ds(k0, width), :]
        s = _nt_dot(q, k)
        if masked:
            q_pos = qb * QBLOCK + (lax.broadcasted_iota(jnp.int32, (rows, width), 0) & (QBLOCK - 1))
            k_pos = k0 + lax.broadcasted_iota(jnp.int32, (rows, width), 1)
            s = jnp.where(k_pos <= q_pos, s, NEG_INF)
        _softmax_update(s, k[:, :KV_RANK], m_ref, l_ref, acc_ref)

    n_full = (qb * QBLOCK) // kc

    def body(i, carry):
        chunk(pl.multiple_of(i * kc, kc), kc, False)
        return carry

    lax.fori_loop(0, n_full, body, 0)
    k_diag = pl.multiple_of(n_full * kc, kc)
    blocks_in = qb - n_full * (kc // QBLOCK)
    for w in range(kc // QBLOCK):
        @pl.when(blocks_in == w)
        def _():
            chunk(k_diag, (w + 1) * QBLOCK, True)

    o = _softmax_result(l_ref, acc_ref)
    for h in range(HEADS):
        sl = slice(h * V_DIM, (h + 1) * V_DIM)
        r = jnp.dot(o[h * QBLOCK:(h + 1) * QBLOCK].astype(BF16), wv_ref[h], preferred_element_type=F32)
        o_ref[0, :, sl] = (r * sz_ref[0, :, sl].astype(F32)).astype(BF16)


def mla_prefill(q, kcat, sz, w_v, zb_col_block):
    b, l, _ = kcat.shape
    nq = l // QBLOCK
    kc = _tile(l, 512)
    rows = HEADS * QBLOCK
    return pl.pallas_call(
        functools.partial(_prefill_kernel, kc=kc),
        grid=(b, nq),
        in_specs=[pl.BlockSpec((1, HEADS, QBLOCK, K_CAT), lambda i, j: (i * nq + j, 0, 0, 0)),
                  pl.BlockSpec((1, l, K_CAT), lambda i, j: (i, 0, 0)),
                  pl.BlockSpec((1, QBLOCK, W_B), lambda i, j: (i, j, zb_col_block)),
                  pl.BlockSpec(w_v.shape, lambda i, j: (0, 0, 0))],
        out_specs=pl.BlockSpec((1, QBLOCK, W_B), lambda i, j: (i, j, 0)),
        out_shape=jax.ShapeDtypeStruct((b, l, W_B), BF16),
        scratch_shapes=[pltpu.VMEM((rows, LANES), F32), pltpu.VMEM((rows, LANES), F32),
                        pltpu.VMEM((rows, KV_RANK), F32)],
        compiler_params=_params(2),
        name="mla_prefill",
    )(q, kcat, sz, w_v)


def _decode_kernel(pt_ref, q_ref, knew_ref, cc_hbm, cr_hbm, o_ref,
                   cbuf, rbuf, cb_ref, rb_ref, m_ref, l_ref, acc_ref, sem_c, sem_r,
                   *, layer, n_pages, ch, lq):
    b = pl.program_id(0)
    n_chunks = n_pages // ch
    total = pl.num_programs(0) * n_chunks
    ahead = DECODE_SLOTS - 1

    def chunk_copies(g):
        slot = lax.rem(g, DECODE_SLOTS)
        copies = []
        for p in range(ch):
            page = pt_ref[g * ch + p]
            copies.append(pltpu.make_async_copy(cc_hbm.at[layer, page], cbuf.at[slot, pl.ds(p * PAGE, PAGE)],
                                                sem_c.at[slot]))
            copies.append(pltpu.make_async_copy(cr_hbm.at[layer, page], rbuf.at[slot, p], sem_r.at[slot]))
        return copies

    def start_chunk(g):
        for n, cp in enumerate(chunk_copies(g)):
            cp.start(priority=(n // 2) % 2)

    @pl.when(b == 0)
    def _():
        for g in range(ahead):
            start_chunk(g)

    _softmax_init(m_ref, l_ref, acc_ref)
    q = q_ref[0]

    def chunk_body(j, carry):
        g = b * n_chunks + j

        @pl.when(g + ahead < total)
        def _():
            start_chunk(g + ahead)

        for cp in chunk_copies(g):
            cp.wait()
        slot = lax.rem(g, DECODE_SLOTS)
        cb_ref[...] = cbuf[slot].astype(BF16)
        for p in range(ch):
            rb_ref[:, p * PAGE:(p + 1) * PAGE] = rbuf[slot, p].astype(BF16)
        cb = cb_ref[...]
        s = _nt_dot(q[:, :KV_RANK], cb) + jnp.dot(q[:, KV_RANK:], rb_ref[...], preferred_element_type=F32)
        _softmax_update(s, cb, m_ref, l_ref, acc_ref)
        return carry

    lax.fori_loop(0, n_chunks, chunk_body, 0)

    kn = knew_ref[0]
    rows = q.shape[0]
    s_new = _nt_dot(q, kn)
    q_pos = lax.broadcasted_iota(jnp.int32, (rows, PAGE), 0) & (lq - 1)
    k_pos = lax.broadcasted_iota(jnp.int32, (rows, PAGE), 1)
    s_new = jnp.where(k_pos <= q_pos, s_new, NEG_INF)
    _softmax_update(s_new, kn[:, :KV_RANK], m_ref, l_ref, acc_ref)
    o_ref[0] = _softmax_result(l_ref, acc_ref).astype(BF16)


def mla_decode(q, knew, cache_c, cache_rt, layer, page_table):
    bd, rows, _ = q.shape
    lq = rows // HEADS
    assert lq & (lq - 1) == 0
    n_pages = page_table.shape[1]
    ch = DECODE_PAGES_PER_CHUNK if n_pages % DECODE_PAGES_PER_CHUNK == 0 else n_pages
    assert bd * (n_pages // ch) >= DECODE_SLOTS
    seq_block = lambda b, pt: (b, 0, 0)
    grid_spec = pltpu.PrefetchScalarGridSpec(
        num_scalar_prefetch=1,
        grid=(bd,),
        in_specs=[pl.BlockSpec((1, rows, K_CAT), seq_block),
                  pl.BlockSpec((1, PAGE, K_CAT), seq_block),
                  pl.BlockSpec(memory_space=pl.ANY),
                  pl.BlockSpec(memory_space=pl.ANY)],
        out_specs=pl.BlockSpec((1, rows, KV_RANK), seq_block),
        scratch_shapes=[pltpu.VMEM((DECODE_SLOTS, ch * PAGE, KV_RANK), F32),
                        pltpu.VMEM((DECODE_SLOTS, ch, QK_ROPE, PAGE), F32),
                        pltpu.VMEM((ch * PAGE, KV_RANK), BF16), pltpu.VMEM((QK_ROPE, ch * PAGE), BF16),
                        pltpu.VMEM((rows, LANES), F32), pltpu.VMEM((rows, LANES), F32),
                        pltpu.VMEM((rows, KV_RANK), F32),
                        pltpu.SemaphoreType.DMA((DECODE_SLOTS,)), pltpu.SemaphoreType.DMA((DECODE_SLOTS,))],
    )
    return pl.pallas_call(
        functools.partial(_decode_kernel, layer=layer, n_pages=n_pages, ch=ch, lq=lq),
        grid_spec=grid_spec,
        out_shape=jax.ShapeDtypeStruct((bd, rows, KV_RANK), BF16),
        compiler_params=_params_seq(("arbitrary",)),
        name="mla_decode",
    )(page_table.reshape(-1), q, knew, cache_c, cache_rt)


def _v_up_kernel(o_ref, wv_ref, sz_ref, out_ref):
    r = jnp.dot(o_ref[0], wv_ref[0], preferred_element_type=F32)
    out_ref[...] = (r * sz_ref[...].astype(F32)).astype(BF16)


def v_up(o_heads, w_v, sz, zb_col0):
    _, m, _ = o_heads.shape
    jb = zb_col0 // V_DIM
    return pl.pallas_call(
        _v_up_kernel,
        grid=(HEADS,),
        in_specs=[pl.BlockSpec((1, m, KV_RANK), lambda h: (h, 0, 0)),
                  pl.BlockSpec((1, KV_RANK, V_DIM), lambda h: (h, 0, 0)),
                  pl.BlockSpec((m, V_DIM), lambda h: (0, h + jb))],
        out_specs=pl.BlockSpec((m, V_DIM), lambda h: (0, h)),
        out_shape=jax.ShapeDtypeStruct((m, W_B), BF16),
        compiler_params=_params(1),
        name="v_up",
    )(o_heads, w_v, sz)


def _conv_kernel(prev_ref, g_ref, sz_ref, w_ref, b_ref, lg_ref, lb_ref, o_ref, xp_ref, xs_ref, d_ref,
                 *, tl, lane_chunk):
    c = g_ref.shape[-1]
    xp_ref[:CONV_HALO, :] = jnp.where(pl.program_id(1) == 0, 0.0, prev_ref[0])
    xp_ref[CONV_HALO:CONV_HALO + tl, :] = g_ref[0]
    base = CONV_HALO - (CONV_W - 1)

    def lane_body(ci, carry):
        c0 = pl.multiple_of(ci * lane_chunk, lane_chunk)
        lanes = pl.ds(c0, lane_chunk)
        for phase in range(1, SUBLANES):
            xs_ref[phase] = xp_ref[phase:phase + tl + CONV_HALO - SUBLANES, lanes]
        acc = jnp.broadcast_to(b_ref[:, lanes], (tl, lane_chunk))
        for k in range(CONV_W):
            phase = (base + k) % SUBLANES
            off = (base + k) - phase
            if phase == 0:
                x = xp_ref[off:off + tl, lanes]
            else:
                x = xs_ref[phase, off:off + tl, :]
            acc = acc + w_ref[k:k + 1, lanes] * x
        d_ref[:, lanes] = acc
        return carry

    lax.fori_loop(0, c // lane_chunk, lane_body, 0)

    rows = min(tl, LN_ROWS)

    def norm_body(ri, carry):
        r = pl.ds(pl.multiple_of(ri * rows, rows), rows)
        y = _layer_norm(d_ref[r, :], lg_ref[...], lb_ref[...])
        o_ref[0, r, :] = (_silu(y) * sz_ref[0, r, :].astype(F32)).astype(BF16)
        return carry

    lax.fori_loop(0, tl // rows, norm_body, 0)


def conv_module(g, sz, w_dw, b_dw, ln_g, ln_b):
    b, l, c = g.shape
    tl = _tile(l, CONV_TILE)
    per = tl // CONV_HALO
    vec = lambda i, j: (0, 0)
    lane_chunk = LANES
    return pl.pallas_call(
        functools.partial(_conv_kernel, tl=tl, lane_chunk=lane_chunk),
        grid=(b, l // tl),
        in_specs=[pl.BlockSpec((1, CONV_HALO, c), lambda i, j: (i, jnp.maximum(j * per - 1, 0), 0)),
                  pl.BlockSpec((1, tl, c), lambda i, j: (i, j, 0)),
                  pl.BlockSpec((1, tl, c), lambda i, j: (i, j, 0)),
                  pl.BlockSpec((CONV_W, c), vec), pl.BlockSpec((1, c), vec),
                  pl.BlockSpec((1, c), vec), pl.BlockSpec((1, c), vec)],
        out_specs=pl.BlockSpec((1, tl, c), lambda i, j: (i, j, 0)),
        out_shape=jax.ShapeDtypeStruct((b, l, c), BF16),
        scratch_shapes=[pltpu.VMEM((CONV_HALO + tl, c), F32),
                        pltpu.VMEM((SUBLANES, tl + CONV_HALO - SUBLANES, lane_chunk), F32),
                        pltpu.VMEM((tl, c), F32)],
        compiler_params=_params(2),
        name="conv_module",
    )(g, g, sz, w_dw, b_dw.reshape(1, c), ln_g.reshape(1, c), ln_b.reshape(1, c))


def _conv_step_kernel(st_ref, g_ref, sz_ref, w_ref, b_ref, lg_ref, lb_ref, o_ref, ns_ref, d_ref, *, lane_chunk):
    n_prev = st_ref.shape[0]
    lq, bb, c = g_ref.shape

    def row(j, lanes):
        return st_ref[j, :, lanes] if j < n_prev else g_ref[j - n_prev, :, lanes]

    def lane_body(ci, carry):
        lanes = pl.ds(pl.multiple_of(ci * lane_chunk, lane_chunk), lane_chunk)
        acc = [jnp.broadcast_to(b_ref[:, lanes], (bb, lane_chunk)) for _ in range(lq)]
        for j in range(n_prev - (CONV_W - 1), n_prev + lq):
            x = row(j, lanes)
            for t in range(lq):
                k = j - (n_prev - (CONV_W - 1)) - t
                if 0 <= k < CONV_W:
                    acc[t] = acc[t] + w_ref[k:k + 1, lanes] * x
        for t in range(lq):
            d_ref[t, :, lanes] = acc[t]
        return carry

    lax.fori_loop(0, c // lane_chunk, lane_body, 0)
    for t in range(lq):
        y = _layer_norm(d_ref[t], lg_ref[...], lb_ref[...])
        o_ref[t] = (_silu(y) * sz_ref[t].astype(F32)).astype(BF16)
    ns_ref[:n_prev - lq] = st_ref[lq:]
    ns_ref[n_prev - lq:] = g_ref[...]


def conv_step(state_t, g_t, sz_t, w_dw, b_dw, ln_g, ln_b):
    n_prev, b, c = state_t.shape
    lq = g_t.shape[0]
    assert n_prev == CONV_W - 1 and lq < n_prev
    bb = _tile(b, 16)
    vec = lambda i: (0, 0)
    blk = lambda i: (0, i, 0)
    return pl.pallas_call(
        functools.partial(_conv_step_kernel, lane_chunk=min(c, 4 * LANES)),
        grid=(b // bb,),
        in_specs=[pl.BlockSpec((n_prev, bb, c), blk), pl.BlockSpec((lq, bb, c), blk), pl.BlockSpec((lq, bb, c), blk),
                  pl.BlockSpec((CONV_W, c), vec), pl.BlockSpec((1, c), vec),
                  pl.BlockSpec((1, c), vec), pl.BlockSpec((1, c), vec)],
        out_specs=[pl.BlockSpec((lq, bb, c), blk), pl.BlockSpec((n_prev, bb, c), blk)],
        out_shape=[jax.ShapeDtypeStruct((lq, b, c), BF16), jax.ShapeDtypeStruct((n_prev, b, c), F32)],
        scratch_shapes=[pltpu.VMEM((lq, bb, c), F32)],
        compiler_params=_params(1),
        name="conv_step",
    )(state_t, g_t, sz_t, w_dw, b_dw.reshape(1, c), ln_g.reshape(1, c), ln_b.reshape(1, c))


def _rope_cs(length, offset, reps):
    inv = jnp.power(ROPE_THETA, -jnp.arange(0, QK_ROPE, 2, dtype=F32) / QK_ROPE)
    pos = jnp.arange(length, dtype=F32) + offset
    ang = pos[:, None] * inv[None, :]
    cos, sin = jnp.cos(ang), jnp.sin(ang)
    return jnp.tile(jnp.concatenate([cos, cos, -sin, sin], axis=-1), (reps, 1))


def _swap_halves(w):
    half = w.shape[-1] // 2
    return jnp.concatenate([w[..., half:], w[..., :half]], axis=-1)


def _even_weights(w_in, w_q_b, w_kv_b):
    wt = w_in.T
    o_v, o_za = W_A, 2 * W_A
    o_qa = o_za + W_A
    o_kv = o_qa + Q_RANK
    o_kr = o_kv + KV_RANK
    o_zb = o_kr + QK_ROPE
    w_kr = wt[o_kr:o_zb]
    w_kr_sw = jnp.concatenate([w_kr[QK_ROPE // 2:], w_kr[:QK_ROPE // 2]], axis=0)
    w_gate = jnp.concatenate([wt[:o_v], wt[o_za:o_qa], wt[o_zb:]], axis=0).astype(BF16)
    w_qa = wt[o_qa:o_kv].astype(BF16)
    w_kv = jnp.concatenate([wt[o_kv:o_kr], w_kr, w_kr_sw], axis=0).astype(BF16)
    wq = w_q_b.reshape(Q_RANK, HEADS, QK_NOPE + QK_ROPE)
    wq_rope = wq[..., QK_NOPE:]
    w_q = jnp.concatenate([wq[..., :QK_NOPE], wq_rope, _swap_halves(wq_rope)], axis=-1)
    w_q = w_q.reshape(Q_RANK, HEADS * 256).astype(BF16)
    wkvb = w_kv_b.reshape(KV_RANK, HEADS, QK_NOPE + V_DIM)
    w_nope = jnp.transpose(wkvb[..., :QK_NOPE], (1, 2, 0)).astype(BF16)
    w_vup = jnp.transpose(wkvb[..., QK_NOPE:], (1, 0, 2)).astype(BF16)
    return wt, w_gate, w_qa, w_kv, w_q, w_nope, w_vup


def _spatial_weights(w_s, b_s, seq_len):
    cl = min(CHUNK, seq_len)
    reps = CHUNK // cl
    wm = (w_s * jnp.tril(jnp.ones((CHUNK, CHUNK), w_s.dtype)))[:, :cl, :cl]
    eye = jnp.eye(reps, dtype=w_s.dtype)
    wm = jnp.einsum("ab,gts->gatbs", eye, wm).reshape(A_GROUPS, CHUNK, CHUNK).astype(BF16)
    bias = jnp.tile(b_s[:, :cl].T, (reps, 1))
    bias = jnp.repeat(bias, A_GROUP_DIM, axis=1)
    return wm, bias


def _even_layer(x, h, batch, seq_len, offset, ew, w_out, ln_v_g, ln_v_b, w_s, b_s, g_q_a, g_kv_a, g_post, g_next,
                past):
    w_in_t, w_gate, w_qa, w_kv, w_q, w_nope, w_vup = ew
    m = x.shape[0]
    gates = mm_act(h, w_gate, col0=0, ncols=3 * W_A, n_gelu_cols=W_A, out_dtype=BF16,
                   transposed_w=True)
    gv = mm_act(h, w_in_t, col0=W_A, ncols=W_A, n_gelu_cols=W_A, out_dtype=F32, transposed_w=True)
    qn = mm_rms(h, w_qa, g_q_a)
    cs = _rope_cs(seq_len, offset, batch)
    c, k_pe, kcat = mm_kv(h, w_kv, g_kv_a, cs)

    wm, bias = _spatial_weights(w_s, b_s, seq_len)
    blocks_per_seq = max(seq_len // CHUNK, 1)
    out_a, v_open = _mix_a_call(gates, gv, ln_v_g, ln_v_b, wm, bias, blocks_per_seq)

    q = mm_q(qn, w_q, w_nope, cs)
    if past is None:
        out_b = mla_prefill(q, kcat.reshape(batch, seq_len, K_CAT), gates.reshape(batch, seq_len, 3 * W_A),
                            w_vup, zb_col_block=2).reshape(m, W_B)
    else:
        cache_c, cache_rt, layer, page_table = past
        per = QBLOCK // seq_len
        qd = q.reshape(m // QBLOCK, HEADS, per, seq_len, K_CAT)
        qd = jnp.transpose(qd, (0, 2, 1, 3, 4)).reshape(batch, HEADS * seq_len, K_CAT)
        knew = jnp.pad(kcat.reshape(batch, seq_len, K_CAT), ((0, 0), (0, PAGE - seq_len), (0, 0)))
        o_lat = mla_decode(qd, knew, cache_c, cache_rt, layer, page_table)
        o_heads = jnp.transpose(o_lat.reshape(batch, HEADS, seq_len, KV_RANK), (1, 0, 2, 3))
        out_b = v_up(o_heads.reshape(HEADS, m, KV_RANK), w_vup, gates, zb_col0=2 * W_A)
    y = mm_plain([out_a, out_b], w_out, BF16)
    x_new, h_next = residual_norm(x, y, g_post, g_next)
    return x_new, h_next, c, k_pe, v_open


def _mix_a_call(gates, gv, ln_v_g, ln_v_b, wm, bias, blocks_per_seq):
    m = gv.shape[0]
    nblk = m // CHUNK
    row0 = lambda i: (i, 0)
    row1 = lambda i: (i, 1)
    fixed2 = lambda i: (0, 0)
    return pl.pallas_call(
        _mix_a_kernel,
        grid=(nblk,),
        in_specs=[pl.BlockSpec((CHUNK, W_A), row0), pl.BlockSpec((CHUNK, W_A), row0),
                  pl.BlockSpec((CHUNK, W_A), row1),
                  pl.BlockSpec((1, W_A), fixed2), pl.BlockSpec((1, W_A), fixed2),
                  pl.BlockSpec((A_GROUPS, CHUNK, CHUNK), lambda i: (0, 0, 0)),
                  pl.BlockSpec((CHUNK, W_A), fixed2)],
        out_specs=[pl.BlockSpec((CHUNK, W_A), row0),
                   pl.BlockSpec((CHUNK, W_A), lambda i: (i // blocks_per_seq, 0))],
        out_shape=[jax.ShapeDtypeStruct((m, W_A), BF16),
                   jax.ShapeDtypeStruct((m // blocks_per_seq, W_A), F32)],
        compiler_params=_params_seq(("arbitrary",)),
        name="mix_a",
    )(gates, gv, gates, ln_v_g.reshape(1, W_A), ln_v_b.reshape(1, W_A), wm, bias)


def _odd_layer(x, h, batch, seq_len, conv_prev, w_in, w_dw, b_dw, ln_c_g, ln_c_b, w_out, g_post, g_next):
    m = x.shape[0]
    c = w_dw.shape[1]
    g = mm_glu(h, w_in, c).reshape(batch, seq_len, c)
    sz = mm_act(h, w_in, col0=2 * c, ncols=c, n_gelu_cols=0, out_dtype=BF16).reshape(batch, seq_len, c)
    if conv_prev is None:
        y2 = conv_module(g, sz, w_dw, b_dw, ln_c_g, ln_c_b)
        new_state = g[:, seq_len - (CONV_W - 1):]
    else:
        tm = lambda a: jnp.swapaxes(a, 0, 1)
        y2, new_state = conv_step(tm(conv_prev), tm(g), tm(sz), w_dw, b_dw, ln_c_g, ln_c_b)
        y2, new_state = tm(y2), tm(new_state)
    y = mm_plain([y2.reshape(m, c)], w_out, BF16)
    x_new, h_next = residual_norm(x, y, g_post, g_next)
    return x_new, h_next, new_state


def kernel(x_prompt, x_sample, cache_kv_latent, cache_k_rope, state_conv, page_table, w_in_even, ln_v_g, ln_v_b,
           w_s, b_s, g_q_a, w_q_b, g_kv_a, w_kv_b, w_out_even, w_in_odd, w_dw, b_dw, ln_c_g, ln_c_b, w_out_odd,
           g_pre, g_post):
    bp, lp, d = x_prompt.shape
    bs, ls, _ = x_sample.shape
    past_len = page_table.shape[1] * PAGE
    depth = g_pre.shape[0]
    xp = x_prompt.reshape(bp * lp, d)
    xs = x_sample.reshape(bs * ls, d)
    lat_p, kpe_p, v_p, conv_p = [], [], [], []
    lat_s, kpe_s, v_s, conv_s = [], [], [], []
    hp = rmsnorm_cast(xp, g_pre[0])
    hs = rmsnorm_cast(xs, g_pre[0])
    for i in range(depth):
        j = i // 2
        g_next = g_pre[i + 1] if i + 1 < depth else None
        if i % 2 == 0:
            ew = _even_weights(w_in_even[j], w_q_b[j], w_kv_b[j])
            args = (ew, w_out_even[j], ln_v_g[j], ln_v_b[j], w_s[j], b_s[j], g_q_a[j], g_kv_a[j], g_post[i], g_next)
            xp, hp, c, k, v = _even_layer(xp, hp, bp, lp, 0, *args, None)
            lat_p.append(c.reshape(bp, lp, KV_RANK))
            kpe_p.append(k.reshape(bp, lp, QK_ROPE))
            v_p.append(v.reshape(bp, -1, W_A))
            cache_rt = jnp.swapaxes(cache_k_rope, 2, 3)
            xs, hs, c, k, v = _even_layer(xs, hs, bs, ls, past_len, *args,
                                          (cache_kv_latent, cache_rt, j, page_table))
            lat_s.append(c.reshape(bs, ls, KV_RANK))
            kpe_s.append(k.reshape(bs, ls, QK_ROPE))
            v_s.append(v.reshape(bs, ls, W_A))
        else:
            args = (w_in_odd[j], w_dw[j], b_dw[j], ln_c_g[j], ln_c_b[j], w_out_odd[j], g_post[i], g_next)
            xp, hp, new_state = _odd_layer(xp, hp, bp, lp, None, *args)
            conv_p.append(new_state)
            xs, hs, new_state = _odd_layer(xs, hs, bs, ls, state_conv[j], *args)
            conv_s.append(new_state)
    return (xp.reshape(bp, lp, d), xs.reshape(bs, ls, d), jnp.stack(lat_p), jnp.stack(kpe_p), jnp.stack(v_p),
            jnp.stack(conv_p), jnp.stack(lat_s), jnp.stack(kpe_s), jnp.stack(v_s), jnp.stack(conv_s))
```

```python
import functools
import math

import jax
import jax.numpy as jnp
from jax import lax
from jax.experimental import pallas as pl
from jax.experimental.pallas import tpu as pltpu

F32 = jnp.float32
BF16 = jnp.bfloat16

EPS = 1e-6
LANES = 128
SUBLANES = 8
CHUNK = 128
A_GROUPS = 16
A_GROUP_DIM = 128
W_A = A_GROUPS * A_GROUP_DIM
HEADS = 16
QK_NOPE = 128
QK_ROPE = 64
V_DIM = 128
Q_RANK = 1024
KV_RANK = 512
W_B = HEADS * V_DIM
K_CAT = KV_RANK + QK_ROPE
ROPE_THETA = 10000.0
QBLOCK = 128
PAGE = 128
DECODE_PAGES_PER_CHUNK = 16
DECODE_SLOTS = 4
CONV_W = 31
CONV_HALO = 32
CONV_TILE = 128
LN_ROWS = 32
ATTN_SCALE = 1.0 / math.sqrt(QK_NOPE + QK_ROPE)
NEG_INF = float("-inf")
VMEM_LIMIT_BYTES = 56 * 1024 * 1024


def _params(n_axes):
    return pltpu.CompilerParams(dimension_semantics=("parallel",) * n_axes,
                                vmem_limit_bytes=VMEM_LIMIT_BYTES)


def _params_seq(sem):
    return pltpu.CompilerParams(dimension_semantics=sem, vmem_limit_bytes=VMEM_LIMIT_BYTES)


def _sigmoid(x):
    return 1.0 / (1.0 + jnp.exp(-x))


def _silu(x):
    return x * _sigmoid(x)


def _rms(x, g):
    return x * lax.rsqrt(jnp.mean(x * x, axis=-1, keepdims=True) + EPS) * g


def _layer_norm(x, g, b):
    mu = jnp.mean(x, axis=-1, keepdims=True)
    xc = x - mu
    return xc * lax.rsqrt(jnp.mean(xc * xc, axis=-1, keepdims=True) + EPS) * g + b


def _tile(n, pref):
    return pref if n % pref == 0 else n


def _rmsnorm_kernel(x_ref, g_ref, o_ref):
    o_ref[...] = _rms(x_ref[...], g_ref[...]).astype(o_ref.dtype)


def rmsnorm_cast(x, g):
    m, d = x.shape
    tm = _tile(m, 256)
    return pl.pallas_call(
        _rmsnorm_kernel,
        grid=(m // tm,),
        in_specs=[pl.BlockSpec((tm, d), lambda i: (i, 0)), pl.BlockSpec((1, d), lambda i: (0, 0))],
        out_specs=pl.BlockSpec((tm, d), lambda i: (i, 0)),
        out_shape=jax.ShapeDtypeStruct((m, d), BF16),
        compiler_params=_params(1),
        name="rmsnorm_cast",
    )(x, g.reshape(1, d))


def _residual_norm_kernel(x_ref, y_ref, g_ref, o_ref):
    o_ref[...] = x_ref[...] + _rms(y_ref[...].astype(F32), g_ref[...])


def _residual_norm_next_kernel(x_ref, y_ref, g_ref, gn_ref, o_ref, h_ref):
    x_new = x_ref[...] + _rms(y_ref[...].astype(F32), g_ref[...])
    o_ref[...] = x_new
    h_ref[...] = _rms(x_new, gn_ref[...]).astype(h_ref.dtype)


def residual_norm(x, y, g, g_next=None):
    m, d = x.shape
    tm = _tile(m, 256)
    row = pl.BlockSpec((tm, d), lambda i: (i, 0))
    vec = pl.BlockSpec((1, d), lambda i: (0, 0))
    if g_next is None:
        return pl.pallas_call(
            _residual_norm_kernel,
            grid=(m // tm,),
            in_specs=[row, row, vec],
            out_specs=row,
            out_shape=jax.ShapeDtypeStruct((m, d), F32),
            compiler_params=_params(1),
            name="residual_norm",
        )(x, y, g.reshape(1, d)), None
    return pl.pallas_call(
        _residual_norm_next_kernel,
        grid=(m // tm,),
        in_specs=[row, row, vec, vec],
        out_specs=[row, row],
        out_shape=[jax.ShapeDtypeStruct((m, d), F32), jax.ShapeDtypeStruct((m, d), BF16)],
        compiler_params=_params(1),
        name="residual_norm_next",
    )(x, y, g.reshape(1, d), g_next.reshape(1, d))


ROW_TILE = 1024


def _bf16_dot(x, w, transposed_w=False):
    w = w.astype(BF16)
    if transposed_w:
        return _nt_dot(x, w)
    return jnp.dot(x, w, preferred_element_type=F32)


def _mm_act_kernel(x_ref, w_ref, o_ref, *, n_gelu_tiles, transposed_w):
    acc = _bf16_dot(x_ref[...], w_ref[...], transposed_w)
    j = pl.program_id(1)

    @pl.when(j < n_gelu_tiles)
    def _():
        o_ref[...] = jax.nn.gelu(acc).astype(o_ref.dtype)

    @pl.when(j >= n_gelu_tiles)
    def _():
        o_ref[...] = _silu(acc).astype(o_ref.dtype)


def mm_act(x, w, *, col0, ncols, n_gelu_cols, out_dtype, transposed_w=False):
    m, k = x.shape
    tm = _tile(m, ROW_TILE)
    tn = _tile(ncols, 512)
    assert col0 % tn == 0 and n_gelu_cols % tn == 0
    j0 = col0 // tn
    if transposed_w:
        w_spec = pl.BlockSpec((tn, k), lambda i, j: (j + j0, 0))
    else:
        w_spec = pl.BlockSpec((k, tn), lambda i, j: (0, j + j0))
    return pl.pallas_call(
        functools.partial(_mm_act_kernel, n_gelu_tiles=n_gelu_cols // tn, transposed_w=transposed_w),
        grid=(m // tm, ncols // tn),
        in_specs=[pl.BlockSpec((tm, k), lambda i, j: (i, 0)), w_spec],
        out_specs=pl.BlockSpec((tm, tn), lambda i, j: (i, j)),
        out_shape=jax.ShapeDtypeStruct((m, ncols), out_dtype),
        compiler_params=_params(2),
        name="mm_act",
    )(x, w)


def _mm_gates_kernel(x_ref, w_ref, o_ref, *, per_gate):
    acc = _bf16_dot(x_ref[...], w_ref[...], True)

    @pl.when(pl.program_id(1) < per_gate)
    def _():
        o_ref[...] = jax.nn.gelu(acc).astype(o_ref.dtype)

    @pl.when(pl.program_id(1) >= per_gate)
    def _():
        o_ref[...] = _silu(acc).astype(o_ref.dtype)


def mm_gates(x, w_in_t, row_starts):
    m, k = x.shape
    tm = _tile(m, ROW_TILE)
    tn = _tile(W_A, 512)
    per_gate = W_A // tn
    u0, za0, zb0 = row_starts

    unit = QK_ROPE
    assert all(r % unit == 0 for r in row_starts) and tn % unit == 0

    def w_map(i, j):
        start = jnp.where(j < per_gate, u0 // unit,
                          jnp.where(j < 2 * per_gate, (za0 - W_A) // unit, (zb0 - 2 * W_A) // unit))
        return ((start + j * (tn // unit)) * unit, 0)

    return pl.pallas_call(
        functools.partial(_mm_gates_kernel, per_gate=per_gate),
        grid=(m // tm, 3 * per_gate),
        in_specs=[pl.BlockSpec((tm, k), lambda i, j: (i, 0)),
                  pl.BlockSpec((pl.Element(tn), pl.Element(k)), w_map)],
        out_specs=pl.BlockSpec((tm, tn), lambda i, j: (i, j)),
        out_shape=jax.ShapeDtypeStruct((m, 3 * W_A), BF16),
        compiler_params=_params(2),
        name="mm_gates",
    )(x, w_in_t)


def _mm_plain_kernel(*refs, n_in):
    x_refs = refs[:n_in]
    w_refs = refs[n_in:2 * n_in]
    o_ref = refs[2 * n_in]
    acc = _bf16_dot(x_refs[0][...], w_refs[0][...])
    for x_ref, w_ref in zip(x_refs[1:], w_refs[1:]):
        acc += _bf16_dot(x_ref[...], w_ref[...])
    o_ref[...] = acc.astype(o_ref.dtype)


def mm_plain(xs, w, out_dtype):
    m = xs[0].shape[0]
    n = w.shape[1]
    tm = _tile(m, ROW_TILE)
    tn = _tile(n, 512)
    in_specs = [pl.BlockSpec((tm, x.shape[1]), lambda i, j: (i, 0)) for x in xs]
    row = 0
    for x in xs:
        kx = x.shape[1]
        assert row % kx == 0
        in_specs.append(pl.BlockSpec((kx, tn), functools.partial(lambda i, j, r: (r, j), r=row // kx)))
        row += kx
    assert row == w.shape[0]
    return pl.pallas_call(
        functools.partial(_mm_plain_kernel, n_in=len(xs)),
        grid=(m // tm, n // tn),
        in_specs=in_specs,
        out_specs=pl.BlockSpec((tm, tn), lambda i, j: (i, j)),
        out_shape=jax.ShapeDtypeStruct((m, n), out_dtype),
        compiler_params=_params(2),
        name="mm_plain",
    )(*xs, *([w] * len(xs)))


def _mm_rms_kernel(x_ref, w_ref, g_ref, o_ref):
    acc = _nt_dot(x_ref[...], w_ref[...])
    o_ref[...] = _rms(acc, g_ref[...]).astype(o_ref.dtype)


def mm_rms(x, w_t, g):
    m, k = x.shape
    n = w_t.shape[0]
    tm = _tile(m, 1024)
    return pl.pallas_call(
        _mm_rms_kernel,
        grid=(m // tm,),
        in_specs=[pl.BlockSpec((tm, k), lambda i: (i, 0)), pl.BlockSpec((n, k), lambda i: (0, 0)),
                  pl.BlockSpec((1, n), lambda i: (0, 0))],
        out_specs=pl.BlockSpec((tm, n), lambda i: (i, 0)),
        out_shape=jax.ShapeDtypeStruct((m, n), BF16),
        compiler_params=_params(1),
        name="mm_rms",
    )(x, w_t, g.reshape(1, n))


def _rope_from_pair(pair, cs):
    t = pair * cs
    return (t + pltpu.roll(t, QK_ROPE, axis=1))[:, :QK_ROPE]


def _mm_kv_kernel(x_ref, w_ref, g_ref, cs_ref, c_ref, kpe_ref, kcat_ref):
    acc = _nt_dot(x_ref[...], w_ref[...])
    c = _rms(acc[:, :KV_RANK], g_ref[...])
    kpe = _rope_from_pair(acc[:, KV_RANK:], cs_ref[...])
    c_ref[...] = c
    kpe_ref[...] = kpe
    kcat_ref[:, :KV_RANK] = c.astype(BF16)
    kcat_ref[:, KV_RANK:] = kpe.astype(BF16)


def mm_kv(x, w_kv, g_kv, cs):
    m, k = x.shape
    n = w_kv.shape[0]
    tm = _tile(m, 1024)
    return pl.pallas_call(
        _mm_kv_kernel,
        grid=(m // tm,),
        in_specs=[pl.BlockSpec((tm, k), lambda i: (i, 0)), pl.BlockSpec((n, k), lambda i: (0, 0)),
                  pl.BlockSpec((1, KV_RANK), lambda i: (0, 0)), pl.BlockSpec((tm, LANES), lambda i: (i, 0))],
        out_specs=[pl.BlockSpec((tm, KV_RANK), lambda i: (i, 0)), pl.BlockSpec((tm, QK_ROPE), lambda i: (i, 0)),
                   pl.BlockSpec((tm, K_CAT), lambda i: (i, 0))],
        out_shape=[jax.ShapeDtypeStruct((m, KV_RANK), F32), jax.ShapeDtypeStruct((m, QK_ROPE), F32),
                   jax.ShapeDtypeStruct((m, K_CAT), BF16)],
        compiler_params=_params(1),
        name="mm_kv",
    )(x, w_kv, g_kv.reshape(1, KV_RANK), cs)


def _mm_q_kernel(x_ref, wq_ref, wn_ref, cs_ref, o_ref):
    x = x_ref[...]
    cs = cs_ref[...]
    groups = x.shape[0] // QBLOCK
    q_all = jnp.dot(x, wq_ref[...], preferred_element_type=F32)
    for h in range(HEADS):
        qh = q_all[:, h * 256:(h + 1) * 256]
        q_abs = jnp.dot(qh[:, :QK_NOPE].astype(BF16), wn_ref[h], preferred_element_type=F32) * ATTN_SCALE
        q_pe = _rope_from_pair(qh[:, QK_NOPE:], cs) * ATTN_SCALE
        o_ref[:, h, :, :KV_RANK] = q_abs.astype(BF16).reshape(groups, QBLOCK, KV_RANK)
        o_ref[:, h, :, KV_RANK:] = q_pe.astype(BF16).reshape(groups, QBLOCK, QK_ROPE)


def mm_q(qn, w_q, w_nope, cs):
    m, k = qn.shape
    tm = _tile(m, 512)
    return pl.pallas_call(
        _mm_q_kernel,
        grid=(m // tm,),
        in_specs=[pl.BlockSpec((tm, k), lambda i: (i, 0)),
                  pl.BlockSpec(w_q.shape, lambda i: (0, 0)),
                  pl.BlockSpec(w_nope.shape, lambda i: (0, 0, 0)),
                  pl.BlockSpec((tm, LANES), lambda i: (i, 0))],
        out_specs=pl.BlockSpec((tm // QBLOCK, HEADS, QBLOCK, K_CAT), lambda i: (i, 0, 0, 0)),
        out_shape=jax.ShapeDtypeStruct((m // QBLOCK, HEADS, QBLOCK, K_CAT), BF16),
        compiler_params=_params(1),
        name="mm_q",
    )(qn, w_q, w_nope, cs)


def _mm_glu_kernel(x_ref, wa_ref, wg_ref, o_ref):
    x = x_ref[...]
    a = _bf16_dot(x, wa_ref[...])
    b = _bf16_dot(x, wg_ref[...])
    o_ref[...] = a * _sigmoid(b)


def mm_glu(x, w, width):
    m, k = x.shape
    tm = _tile(m, ROW_TILE)
    tn = _tile(width, 256)
    nb = width // tn
    return pl.pallas_call(
        _mm_glu_kernel,
        grid=(m // tm, nb),
        in_specs=[pl.BlockSpec((tm, k), lambda i, j: (i, 0)),
                  pl.BlockSpec((k, tn), lambda i, j: (0, j)),
                  pl.BlockSpec((k, tn), lambda i, j: (0, j + nb))],
        out_specs=pl.BlockSpec((tm, tn), lambda i, j: (i, j)),
        out_shape=jax.ShapeDtypeStruct((m, width), F32),
        compiler_params=_params(2),
        name="mm_glu",
    )(x, w, w)


def _mix_a_kernel(gu_ref, gv_ref, sz_ref, lg_ref, lb_ref, wm_ref, bias_ref, oa_ref, vo_ref):
    vn = _layer_norm(gv_ref[...], lg_ref[...], lb_ref[...])
    vo_ref[...] = vn
    vb = vn.astype(BF16)
    for g in range(A_GROUPS):
        sl = slice(g * A_GROUP_DIM, (g + 1) * A_GROUP_DIM)
        mix = jnp.dot(wm_ref[g], vb[:, sl], preferred_element_type=F32) + bias_ref[:, sl]
        oa_ref[:, sl] = (gu_ref[:, sl].astype(F32) * mix * sz_ref[:, sl].astype(F32)).astype(BF16)


def _softmax_update(s, v, m_ref, l_ref, acc_ref):
    n_rep = s.shape[1] // LANES
    m_prev = m_ref[...]
    m_new = jnp.maximum(m_prev, jnp.max(s, axis=1, keepdims=True))
    alpha = jnp.exp(m_prev - m_new)
    p = jnp.exp(s - jnp.concatenate([m_new] * n_rep, axis=1))
    l_ref[...] = alpha * l_ref[...] + jnp.sum(p, axis=1, keepdims=True)
    m_ref[...] = m_new
    acc_ref[...] = (acc_ref[...] * jnp.concatenate([alpha] * (KV_RANK // LANES), axis=1)
                    + jnp.dot(p.astype(BF16), v, preferred_element_type=F32))


def _softmax_init(m_ref, l_ref, acc_ref):
    m_ref[...] = jnp.full(m_ref.shape, NEG_INF, F32)
    l_ref[...] = jnp.zeros(l_ref.shape, F32)
    acc_ref[...] = jnp.zeros(acc_ref.shape, F32)


def _softmax_result(l_ref, acc_ref):
    inv = 1.0 / l_ref[...]
    return acc_ref[...] * jnp.concatenate([inv] * (KV_RANK // LANES), axis=1)


def _nt_dot(a, b):
    return lax.dot_general(a, b, (((1,), (1,)), ((), ())), preferred_element_type=F32)


def _prefill_kernel(q_ref, k_ref, sz_ref, wv_ref, o_ref, m_ref, l_ref, acc_ref, *, kc):
    qb = pl.program_id(1)
    rows = HEADS * QBLOCK
    q = q_ref[0].reshape(rows, K_CAT)
    _softmax_init(m_ref, l_ref, acc_ref)

    def chunk(k0, width, masked):
        k = k_ref[0, pl.ds(k0, width), :]
        s = _nt_dot(q, k)
        if masked:
            q_pos = qb * QBLOCK + (lax.broadcasted_iota(jnp.int32, (rows, width), 0) & (QBLOCK - 1))
            k_pos = k0 + lax.broadcasted_iota(jnp.int32, (rows, width), 1)
            s = jnp.where(k_pos <= q_pos, s, NEG_INF)
        _softmax_update(s, k[:, :KV_RANK], m_ref, l_ref, acc_ref)

    n_full = (qb * QBLOCK) // kc

    def body(i, carry):
        chunk(pl.multiple_of(i * kc, kc), kc, False)
        return carry

    lax.fori_loop(0, n_full, body, 0)
    k_diag = pl.multiple_of(n_full * kc, kc)
    blocks_in = qb - n_full * (kc // QBLOCK)
    for w in range(kc // QBLOCK):
        @pl.when(blocks_in == w)
        def _():
            chunk(k_diag, (w + 1) * QBLOCK, True)

    o = _softmax_result(l_ref, acc_ref)
    for h in range(HEADS):
        sl = slice(h * V_DIM, (h + 1) * V_DIM)
        r = jnp.dot(o[h * QBLOCK:(h + 1) * QBLOCK].astype(BF16), wv_ref[h], preferred_element_type=F32)
        o_ref[0, :, sl] = (r * sz_ref[0, :, sl].astype(F32)).astype(BF16)


def mla_prefill(q, kcat, sz, w_v, zb_col_block):
    b, l, _ = kcat.shape
    nq = l // QBLOCK
    kc = _tile(l, 512)
    rows = HEADS * QBLOCK
    return pl.pallas_call(
        functools.partial(_prefill_kernel, kc=kc),
        grid=(b, nq),
        in_specs=[pl.BlockSpec((1, HEADS, QBLOCK, K_CAT), lambda i, j: (i * nq + j, 0, 0, 0)),
                  pl.BlockSpec((1, l, K_CAT), lambda i, j: (i, 0, 0)),
                  pl.BlockSpec((1, QBLOCK, W_B), lambda i, j: (i, j, zb_col_block)),
                  pl.BlockSpec(w_v.shape, lambda i, j: (0, 0, 0))],
        out_specs=pl.BlockSpec((1, QBLOCK, W_B), lambda i, j: (i, j, 0)),
        out_shape=jax.ShapeDtypeStruct((b, l, W_B), BF16),
        scratch_shapes=[pltpu.VMEM((rows, LANES), F32), pltpu.VMEM((rows, LANES), F32),
                        pltpu.VMEM((rows, KV_RANK), F32)],
        compiler_params=_params(2),
        name="mla_prefill",
    )(q, kcat, sz, w_v)


def _decode_kernel(pt_ref, q_ref, knew_ref, cc_hbm, cr_hbm, o_ref,
                   cbuf, rbuf, cb_ref, rb_ref, m_ref, l_ref, acc_ref, sem_c, sem_r,
                   *, layer, n_pages, ch, lq):
    b = pl.program_id(0)
    n_chunks = n_pages // ch
    total = pl.num_programs(0) * n_chunks
    ahead = DECODE_SLOTS - 1

    def chunk_copies(g):
        slot = lax.rem(g, DECODE_SLOTS)
        copies = []
        for p in range(ch):
            page = pt_ref[g * ch + p]
            copies.append(pltpu.make_async_copy(cc_hbm.at[layer, page], cbuf.at[slot, pl.ds(p * PAGE, PAGE)],
                                                sem_c.at[slot]))
            copies.append(pltpu.make_async_copy(cr_hbm.at[layer, page], rbuf.at[slot, p], sem_r.at[slot]))
        return copies

    def start_chunk(g):
        for n, cp in enumerate(chunk_copies(g)):
            cp.start(priority=(n // 2) % 2)

    @pl.when(b == 0)
    def _():
        for g in range(ahead):
            start_chunk(g)

    _softmax_init(m_ref, l_ref, acc_ref)
    q = q_ref[0]

    def chunk_body(j, carry):
        g = b * n_chunks + j

        @pl.when(g + ahead < total)
        def _():
            start_chunk(g + ahead)

        for cp in chunk_copies(g):
            cp.wait()
        slot = lax.rem(g, DECODE_SLOTS)
        cb_ref[...] = cbuf[slot].astype(BF16)
        for p in range(ch):
            rb_ref[:, p * PAGE:(p + 1) * PAGE] = rbuf[slot, p].astype(BF16)
        cb = cb_ref[...]
        s = _nt_dot(q[:, :KV_RANK], cb) + jnp.dot(q[:, KV_RANK:], rb_ref[...], preferred_element_type=F32)
        _softmax_update(s, cb, m_ref, l_ref, acc_ref)
        return carry

    lax.fori_loop(0, n_chunks, chunk_body, 0)

    kn = knew_ref[0]
    rows = q.shape[0]
    s_new = _nt_dot(q, kn)
    q_pos = lax.broadcasted_iota(jnp.int32, (rows, PAGE), 0) & (lq - 1)
    k_pos = lax.broadcasted_iota(jnp.int32, (rows, PAGE), 1)
    s_new = jnp.where(k_pos <= q_pos, s_new, NEG_INF)
    _softmax_update(s_new, kn[:, :KV_RANK], m_ref, l_ref, acc_ref)
    o_ref[0] = _softmax_result(l_ref, acc_ref).astype(BF16)


def mla_decode(q, knew, cache_c, cache_rt, layer, page_table):
    bd, rows, _ = q.shape
    lq = rows // HEADS
    assert lq & (lq - 1) == 0
    n_pages = page_table.shape[1]
    ch = DECODE_PAGES_PER_CHUNK if n_pages % DECODE_PAGES_PER_CHUNK == 0 else n_pages
    assert bd * (n_pages // ch) >= DECODE_SLOTS
    seq_block = lambda b, pt: (b, 0, 0)
    grid_spec = pltpu.PrefetchScalarGridSpec(
        num_scalar_prefetch=1,
        grid=(bd,),
        in_specs=[pl.BlockSpec((1, rows, K_CAT), seq_block),
                  pl.BlockSpec((1, PAGE, K_CAT), seq_block),
                  pl.BlockSpec(memory_space=pl.ANY),
                  pl.BlockSpec(memory_space=pl.ANY)],
        out_specs=pl.BlockSpec((1, rows, KV_RANK), seq_block),
        scratch_shapes=[pltpu.VMEM((DECODE_SLOTS, ch * PAGE, KV_RANK), F32),
                        pltpu.VMEM((DECODE_SLOTS, ch, QK_ROPE, PAGE), F32),
                        pltpu.VMEM((ch * PAGE, KV_RANK), BF16), pltpu.VMEM((QK_ROPE, ch * PAGE), BF16),
                        pltpu.VMEM((rows, LANES), F32), pltpu.VMEM((rows, LANES), F32),
                        pltpu.VMEM((rows, KV_RANK), F32),
                        pltpu.SemaphoreType.DMA((DECODE_SLOTS,)), pltpu.SemaphoreType.DMA((DECODE_SLOTS,))],
    )
    return pl.pallas_call(
        functools.partial(_decode_kernel, layer=layer, n_pages=n_pages, ch=ch, lq=lq),
        grid_spec=grid_spec,
        out_shape=jax.ShapeDtypeStruct((bd, rows, KV_RANK), BF16),
        compiler_params=_params_seq(("arbitrary",)),
        name="mla_decode",
    )(page_table.reshape(-1), q, knew, cache_c, cache_rt)


def _v_up_kernel(o_ref, wv_ref, sz_ref, out_ref):
    r = jnp.dot(o_ref[0], wv_ref[0], preferred_element_type=F32)
    out_ref[...] = (r * sz_ref[...].astype(F32)).astype(BF16)


def v_up(o_heads, w_v, sz, zb_col0):
    _, m, _ = o_heads.shape
    jb = zb_col0 // V_DIM
    return pl.pallas_call(
        _v_up_kernel,
        grid=(HEADS,),
        in_specs=[pl.BlockSpec((1, m, KV_RANK), lambda h: (h, 0, 0)),
                  pl.BlockSpec((1, KV_RANK, V_DIM), lambda h: (h, 0, 0)),
                  pl.BlockSpec((m, V_DIM), lambda h: (0, h + jb))],
        out_specs=pl.BlockSpec((m, V_DIM), lambda h: (0, h)),
        out_shape=jax.ShapeDtypeStruct((m, W_B), BF16),
        compiler_params=_params(1),
        name="v_up",
    )(o_heads, w_v, sz)


def _conv_kernel(prev_ref, g_ref, sz_ref, w_ref, b_ref, lg_ref, lb_ref, o_ref, xp_ref, xs_ref, d_ref,
                 *, tl, lane_chunk):
    c = g_ref.shape[-1]
    xp_ref[:CONV_HALO, :] = jnp.where(pl.program_id(1) == 0, 0.0, prev_ref[0])
    xp_ref[CONV_HALO:CONV_HALO + tl, :] = g_ref[0]
    base = CONV_HALO - (CONV_W - 1)

    def lane_body(ci, carry):
        c0 = pl.multiple_of(ci * lane_chunk, lane_chunk)
        lanes = pl.ds(c0, lane_chunk)
        for phase in range(1, SUBLANES):
            xs_ref[phase] = xp_ref[phase:phase + tl + CONV_HALO - SUBLANES, lanes]
        acc = jnp.broadcast_to(b_ref[:, lanes], (tl, lane_chunk))
        for k in range(CONV_W):
            phase = (base + k) % SUBLANES
            off = (base + k) - phase
            if phase == 0:
                x = xp_ref[off:off + tl, lanes]
            else:
                x = xs_ref[phase, off:off + tl, :]
            acc = acc + w_ref[k:k + 1, lanes] * x
        d_ref[:, lanes] = acc
        return carry

    lax.fori_loop(0, c // lane_chunk, lane_body, 0)

    rows = min(tl, LN_ROWS)

    def norm_body(ri, carry):
        r = pl.ds(pl.multiple_of(ri * rows, rows), rows)
        y = _layer_norm(d_ref[r, :], lg_ref[...], lb_ref[...])
        o_ref[0, r, :] = (_silu(y) * sz_ref[0, r, :].astype(F32)).astype(BF16)
        return carry

    lax.fori_loop(0, tl // rows, norm_body, 0)


def conv_module(g, sz, w_dw, b_dw, ln_g, ln_b):
    b, l, c = g.shape
    tl = _tile(l, CONV_TILE)
    per = tl // CONV_HALO
    vec = lambda i, j: (0, 0)
    lane_chunk = LANES
    return pl.pallas_call(
        functools.partial(_conv_kernel, tl=tl, lane_chunk=lane_chunk),
        grid=(b, l // tl),
        in_specs=[pl.BlockSpec((1, CONV_HALO, c), lambda i, j: (i, jnp.maximum(j * per - 1, 0), 0)),
                  pl.BlockSpec((1, tl, c), lambda i, j: (i, j, 0)),
                  pl.BlockSpec((1, tl, c), lambda i, j: (i, j, 0)),
                  pl.BlockSpec((CONV_W, c), vec), pl.BlockSpec((1, c), vec),
                  pl.BlockSpec((1, c), vec), pl.BlockSpec((1, c), vec)],
        out_specs=pl.BlockSpec((1, tl, c), lambda i, j: (i, j, 0)),
        out_shape=jax.ShapeDtypeStruct((b, l, c), BF16),
        scratch_shapes=[pltpu.VMEM((CONV_HALO + tl, c), F32),
                        pltpu.VMEM((SUBLANES, tl + CONV_HALO - SUBLANES, lane_chunk), F32),
                        pltpu.VMEM((tl, c), F32)],
        compiler_params=_params(2),
        name="conv_module",
    )(g, g, sz, w_dw, b_dw.reshape(1, c), ln_g.reshape(1, c), ln_b.reshape(1, c))


def _conv_step_kernel(st_ref, g_ref, sz_ref, w_ref, b_ref, lg_ref, lb_ref, o_ref, ns_ref, d_ref, *, lane_chunk):
    n_prev = st_ref.shape[0]
    lq, bb, c = g_ref.shape

    def row(j, lanes):
        return st_ref[j, :, lanes] if j < n_prev else g_ref[j - n_prev, :, lanes]

    def lane_body(ci, carry):
        lanes = pl.ds(pl.multiple_of(ci * lane_chunk, lane_chunk), lane_chunk)
        acc = [jnp.broadcast_to(b_ref[:, lanes], (bb, lane_chunk)) for _ in range(lq)]
        for j in range(n_prev - (CONV_W - 1), n_prev + lq):
            x = row(j, lanes)
            for t in range(lq):
                k = j - (n_prev - (CONV_W - 1)) - t
                if 0 <= k < CONV_W:
                    acc[t] = acc[t] + w_ref[k:k + 1, lanes] * x
        for t in range(lq):
            d_ref[t, :, lanes] = acc[t]
        return carry

    lax.fori_loop(0, c // lane_chunk, lane_body, 0)
    for t in range(lq):
        y = _layer_norm(d_ref[t], lg_ref[...], lb_ref[...])
        o_ref[t] = (_silu(y) * sz_ref[t].astype(F32)).astype(BF16)
    ns_ref[:n_prev - lq] = st_ref[lq:]
    ns_ref[n_prev - lq:] = g_ref[...]


def conv_step(state_t, g_t, sz_t, w_dw, b_dw, ln_g, ln_b):
    n_prev, b, c = state_t.shape
    lq = g_t.shape[0]
    assert n_prev == CONV_W - 1 and lq < n_prev
    bb = _tile(b, 16)
    vec = lambda i: (0, 0)
    blk = lambda i: (0, i, 0)
    return pl.pallas_call(
        functools.partial(_conv_step_kernel, lane_chunk=min(c, 4 * LANES)),
        grid=(b // bb,),
        in_specs=[pl.BlockSpec((n_prev, bb, c), blk), pl.BlockSpec((lq, bb, c), blk), pl.BlockSpec((lq, bb, c), blk),
                  pl.BlockSpec((CONV_W, c), vec), pl.BlockSpec((1, c), vec),
                  pl.BlockSpec((1, c), vec), pl.BlockSpec((1, c), vec)],
        out_specs=[pl.BlockSpec((lq, bb, c), blk), pl.BlockSpec((n_prev, bb, c), blk)],
        out_shape=[jax.ShapeDtypeStruct((lq, b, c), BF16), jax.ShapeDtypeStruct((n_prev, b, c), F32)],
        scratch_shapes=[pltpu.VMEM((lq, bb, c), F32)],
        compiler_params=_params(1),
        name="conv_step",
    )(state_t, g_t, sz_t, w_dw, b_dw.reshape(1, c), ln_g.reshape(1, c), ln_b.reshape(1, c))


def _rope_cs(length, offset, reps):
    inv = jnp.power(ROPE_THETA, -jnp.arange(0, QK_ROPE, 2, dtype=F32) / QK_ROPE)
    pos = jnp.arange(length, dtype=F32) + offset
    ang = pos[:, None] * inv[None, :]
    cos, sin = jnp.cos(ang), jnp.sin(ang)
    return jnp.tile(jnp.concatenate([cos, cos, -sin, sin], axis=-1), (reps, 1))


def _swap_halves(w):
    half = w.shape[-1] // 2
    return jnp.concatenate([w[..., half:], w[..., :half]], axis=-1)


def _even_weights(w_in, w_q_b, w_kv_b):
    wt = w_in.T
    o_qa = 3 * W_A
    o_kv = o_qa + Q_RANK
    o_kr = o_kv + KV_RANK
    o_zb = o_kr + QK_ROPE
    w_mid = lax.optimization_barrier(wt[o_qa:o_zb]).astype(BF16)
    w_qa = w_mid[:Q_RANK]
    w_kr = w_mid[Q_RANK + KV_RANK:]
    w_kr_sw = jnp.concatenate([w_kr[QK_ROPE // 2:], w_kr[:QK_ROPE // 2]], axis=0)
    w_kv = jnp.concatenate([w_mid[Q_RANK:], w_kr_sw], axis=0)
    wq = w_q_b.reshape(Q_RANK, HEADS, QK_NOPE + QK_ROPE)
    wq_rope = wq[..., QK_NOPE:]
    w_q = jnp.concatenate([wq[..., :QK_NOPE], wq_rope, _swap_halves(wq_rope)], axis=-1)
    w_q = w_q.reshape(Q_RANK, HEADS * 256).astype(BF16)
    wkvb = w_kv_b.reshape(KV_RANK, HEADS, QK_NOPE + V_DIM)
    w_nope = jnp.transpose(wkvb[..., :QK_NOPE], (1, 2, 0)).astype(BF16)
    w_vup = jnp.transpose(wkvb[..., QK_NOPE:], (1, 0, 2)).astype(BF16)
    return wt, w_qa, w_kv, w_q, w_nope, w_vup


def _spatial_weights(w_s, b_s, seq_len):
    cl = min(CHUNK, seq_len)
    reps = CHUNK // cl
    wm = (w_s * jnp.tril(jnp.ones((CHUNK, CHUNK), w_s.dtype)))[:, :cl, :cl]
    eye = jnp.eye(reps, dtype=w_s.dtype)
    wm = jnp.einsum("ab,gts->gatbs", eye, wm).reshape(A_GROUPS, CHUNK, CHUNK).astype(BF16)
    bias = jnp.tile(b_s[:, :cl].T, (reps, 1))
    bias = jnp.repeat(bias, A_GROUP_DIM, axis=1)
    return wm, bias


def _even_layer(x, h, batch, seq_len, offset, ew, w_out, ln_v_g, ln_v_b, w_s, b_s, g_q_a, g_kv_a, g_post, g_next,
                past):
    w_in_t, w_qa, w_kv, w_q, w_nope, w_vup = ew
    m = x.shape[0]
    z_b_row0 = 3 * W_A + Q_RANK + KV_RANK + QK_ROPE
    gates = mm_gates(h, w_in_t, (0, 2 * W_A, z_b_row0))
    gv = mm_act(h, w_in_t, col0=W_A, ncols=W_A, n_gelu_cols=W_A, out_dtype=F32, transposed_w=True)
    qn = mm_rms(h, w_qa, g_q_a)
    cs = _rope_cs(seq_len, offset, batch)
    c, k_pe, kcat = mm_kv(h, w_kv, g_kv_a, cs)

    wm, bias = _spatial_weights(w_s, b_s, seq_len)
    blocks_per_seq = max(seq_len // CHUNK, 1)
    out_a, v_open = _mix_a_call(gates, gv, ln_v_g, ln_v_b, wm, bias, blocks_per_seq)

    q = mm_q(qn, w_q, w_nope, cs)
    if past is None:
        out_b = mla_prefill(q, kcat.reshape(batch, seq_len, K_CAT), gates.reshape(batch, seq_len, 3 * W_A),
                            w_vup, zb_col_block=2).reshape(m, W_B)
    else:
        cache_c, cache_rt, layer, page_table = past
        per = QBLOCK // seq_len
        qd = q.reshape(m // QBLOCK, HEADS, per, seq_len, K_CAT)
        qd = jnp.transpose(qd, (0, 2, 1, 3, 4)).reshape(batch, HEADS * seq_len, K_CAT)
        knew = jnp.pad(kcat.reshape(batch, seq_len, K_CAT), ((0, 0), (0, PAGE - seq_len), (0, 0)))
        o_lat = mla_decode(qd, knew, cache_c, cache_rt, layer, page_table)
        o_heads = jnp.transpose(o_lat.reshape(batch, HEADS, seq_len, KV_RANK), (1, 0, 2, 3))
        out_b = v_up(o_heads.reshape(HEADS, m, KV_RANK), w_vup, gates, zb_col0=2 * W_A)
    y = mm_plain([out_a, out_b], w_out, BF16)
    x_new, h_next = residual_norm(x, y, g_post, g_next)
    return x_new, h_next, c, k_pe, v_open


def _mix_a_call(gates, gv, ln_v_g, ln_v_b, wm, bias, blocks_per_seq):
    m = gv.shape[0]
    nblk = m // CHUNK
    row0 = lambda i: (i, 0)
    row1 = lambda i: (i, 1)
    fixed2 = lambda i: (0, 0)
    return pl.pallas_call(
        _mix_a_kernel,
        grid=(nblk,),
        in_specs=[pl.BlockSpec((CHUNK, W_A), row0), pl.BlockSpec((CHUNK, W_A), row0),
                  pl.BlockSpec((CHUNK, W_A), row1),
                  pl.BlockSpec((1, W_A), fixed2), pl.BlockSpec((1, W_A), fixed2),
                  pl.BlockSpec((A_GROUPS, CHUNK, CHUNK), lambda i: (0, 0, 0)),
                  pl.BlockSpec((CHUNK, W_A), fixed2)],
        out_specs=[pl.BlockSpec((CHUNK, W_A), row0),
                   pl.BlockSpec((CHUNK, W_A), lambda i: (i // blocks_per_seq, 0))],
        out_shape=[jax.ShapeDtypeStruct((m, W_A), BF16),
                   jax.ShapeDtypeStruct((m // blocks_per_seq, W_A), F32)],
        compiler_params=_params_seq(("arbitrary",)),
        name="mix_a",
    )(gates, gv, gates, ln_v_g.reshape(1, W_A), ln_v_b.reshape(1, W_A), wm, bias)


def _odd_layer(x, h, batch, seq_len, conv_prev, w_in, w_dw, b_dw, ln_c_g, ln_c_b, w_out, g_post, g_next):
    m = x.shape[0]
    c = w_dw.shape[1]
    g = mm_glu(h, w_in, c).reshape(batch, seq_len, c)
    sz = mm_act(h, w_in, col0=2 * c, ncols=c, n_gelu_cols=0, out_dtype=BF16).reshape(batch, seq_len, c)
    if conv_prev is None:
        y2 = conv_module(g, sz, w_dw, b_dw, ln_c_g, ln_c_b)
        new_state = g[:, seq_len - (CONV_W - 1):]
    else:
        tm = lambda a: jnp.swapaxes(a, 0, 1)
        y2, new_state = conv_step(tm(conv_prev), tm(g), tm(sz), w_dw, b_dw, ln_c_g, ln_c_b)
        y2, new_state = tm(y2), tm(new_state)
    y = mm_plain([y2.reshape(m, c)], w_out, BF16)
    x_new, h_next = residual_norm(x, y, g_post, g_next)
    return x_new, h_next, new_state


def kernel(x_prompt, x_sample, cache_kv_latent, cache_k_rope, state_conv, page_table, w_in_even, ln_v_g, ln_v_b,
           w_s, b_s, g_q_a, w_q_b, g_kv_a, w_kv_b, w_out_even, w_in_odd, w_dw, b_dw, ln_c_g, ln_c_b, w_out_odd,
           g_pre, g_post):
    bp, lp, d = x_prompt.shape
    bs, ls, _ = x_sample.shape
    past_len = page_table.shape[1] * PAGE
    depth = g_pre.shape[0]
    xp = x_prompt.reshape(bp * lp, d)
    xs = x_sample.reshape(bs * ls, d)
    lat_p, kpe_p, v_p, conv_p = [], [], [], []
    lat_s, kpe_s, v_s, conv_s = [], [], [], []
    hp = rmsnorm_cast(xp, g_pre[0])
    hs = rmsnorm_cast(xs, g_pre[0])
    for i in range(depth):
        j = i // 2
        g_next = g_pre[i + 1] if i + 1 < depth else None
        if i % 2 == 0:
            ew = _even_weights(w_in_even[j], w_q_b[j], w_kv_b[j])
            args = (ew, w_out_even[j], ln_v_g[j], ln_v_b[j], w_s[j], b_s[j], g_q_a[j], g_kv_a[j], g_post[i], g_next)
            xp, hp, c, k, v = _even_layer(xp, hp, bp, lp, 0, *args, None)
            lat_p.append(c.reshape(bp, lp, KV_RANK))
            kpe_p.append(k.reshape(bp, lp, QK_ROPE))
            v_p.append(v.reshape(bp, -1, W_A))
            cache_rt = jnp.swapaxes(cache_k_rope, 2, 3)
            xs, hs, c, k, v = _even_layer(xs, hs, bs, ls, past_len, *args,
                                          (cache_kv_latent, cache_rt, j, page_table))
            lat_s.append(c.reshape(bs, ls, KV_RANK))
            kpe_s.append(k.reshape(bs, ls, QK_ROPE))
            v_s.append(v.reshape(bs, ls, W_A))
        else:
            args = (w_in_odd[j], w_dw[j], b_dw[j], ln_c_g[j], ln_c_b[j], w_out_odd[j], g_post[i], g_next)
            xp, hp, new_state = _odd_layer(xp, hp, bp, lp, None, *args)
            conv_p.append(new_state)
            xs, hs, new_state = _odd_layer(xs, hs, bs, ls, state_conv[j], *args)
            conv_s.append(new_state)
    return (xp.reshape(bp, lp, d), xs.reshape(bs, ls, d), jnp.stack(lat_p), jnp.stack(kpe_p), jnp.stack(v_p),
            jnp.stack(conv_p), jnp.stack(lat_s), jnp.stack(kpe_s), jnp.stack(v_s), jnp.stack(conv_s))
```

```python
import functools
import math

import jax
import jax.numpy as jnp
from jax import lax
from jax.experimental import pallas as pl
from jax.experimental.pallas import tpu as pltpu

F32 = jnp.float32
BF16 = jnp.bfloat16

EPS = 1e-6
LANES = 128
SUBLANES = 8
CHUNK = 128
A_GROUPS = 16
A_GROUP_DIM = 128
W_A = A_GROUPS * A_GROUP_DIM
HEADS = 16
QK_NOPE = 128
QK_ROPE = 64
V_DIM = 128
Q_RANK = 1024
KV_RANK = 512
W_B = HEADS * V_DIM
K_CAT = KV_RANK + QK_ROPE
ROPE_THETA = 10000.0
QBLOCK = 128
PAGE = 128
DECODE_PAGES_PER_CHUNK = 16
DECODE_SLOTS = 4
CONV_W = 31
CONV_HALO = 32
CONV_TILE = 128
LN_ROWS = 64
ATTN_SCALE = 1.0 / math.sqrt(QK_NOPE + QK_ROPE)
NEG_INF = float("-inf")
VMEM_LIMIT_BYTES = 56 * 1024 * 1024


def _params(n_axes):
    return pltpu.CompilerParams(dimension_semantics=("parallel",) * n_axes,
                                vmem_limit_bytes=VMEM_LIMIT_BYTES)


def _params_seq(sem):
    return pltpu.CompilerParams(dimension_semantics=sem, vmem_limit_bytes=VMEM_LIMIT_BYTES)


def _sigmoid(x):
    return 1.0 / (1.0 + jnp.exp(-x))


def _silu(x):
    return x * _sigmoid(x)


def _rms(x, g):
    return x * lax.rsqrt(jnp.mean(x * x, axis=-1, keepdims=True) + EPS) * g


def _layer_norm(x, g, b):
    mu = jnp.mean(x, axis=-1, keepdims=True)
    xc = x - mu
    return xc * lax.rsqrt(jnp.mean(xc * xc, axis=-1, keepdims=True) + EPS) * g + b


def _tile(n, pref):
    return pref if n % pref == 0 else n


def _rmsnorm_kernel(x_ref, g_ref, o_ref):
    o_ref[...] = _rms(x_ref[...], g_ref[...]).astype(o_ref.dtype)


def rmsnorm_cast(x, g):
    m, d = x.shape
    tm = _tile(m, 256)
    return pl.pallas_call(
        _rmsnorm_kernel,
        grid=(m // tm,),
        in_specs=[pl.BlockSpec((tm, d), lambda i: (i, 0)), pl.BlockSpec((1, d), lambda i: (0, 0))],
        out_specs=pl.BlockSpec((tm, d), lambda i: (i, 0)),
        out_shape=jax.ShapeDtypeStruct((m, d), BF16),
        compiler_params=_params(1),
        name="rmsnorm_cast",
    )(x, g.reshape(1, d))


def _residual_norm_kernel(x_ref, y_ref, g_ref, o_ref):
    o_ref[...] = x_ref[...] + _rms(y_ref[...].astype(F32), g_ref[...])


def _residual_norm_next_kernel(x_ref, y_ref, g_ref, gn_ref, o_ref, h_ref):
    x_new = x_ref[...] + _rms(y_ref[...].astype(F32), g_ref[...])
    o_ref[...] = x_new
    h_ref[...] = _rms(x_new, gn_ref[...]).astype(h_ref.dtype)


def residual_norm(x, y, g, g_next=None):
    m, d = x.shape
    tm = _tile(m, 256)
    row = pl.BlockSpec((tm, d), lambda i: (i, 0))
    vec = pl.BlockSpec((1, d), lambda i: (0, 0))
    if g_next is None:
        return pl.pallas_call(
            _residual_norm_kernel,
            grid=(m // tm,),
            in_specs=[row, row, vec],
            out_specs=row,
            out_shape=jax.ShapeDtypeStruct((m, d), F32),
            compiler_params=_params(1),
            name="residual_norm",
        )(x, y, g.reshape(1, d)), None
    return pl.pallas_call(
        _residual_norm_next_kernel,
        grid=(m // tm,),
        in_specs=[row, row, vec, vec],
        out_specs=[row, row],
        out_shape=[jax.ShapeDtypeStruct((m, d), F32), jax.ShapeDtypeStruct((m, d), BF16)],
        compiler_params=_params(1),
        name="residual_norm_next",
    )(x, y, g.reshape(1, d), g_next.reshape(1, d))


ROW_TILE = 1024


def _col_tile(m, n):
    return _tile(n, 512 if m >= ROW_TILE else 1024)


def _bf16_dot(x, w, transposed_w=False):
    w = w.astype(BF16)
    if transposed_w:
        return _nt_dot(x, w)
    return jnp.dot(x, w, preferred_element_type=F32)


def _act_tile(x_ref, w_ref, o_ref, act, transposed_w):
    o_ref[...] = act(_bf16_dot(x_ref[...], w_ref[...], transposed_w)).astype(o_ref.dtype)


def _mm_act_kernel(x_ref, w_ref, o_ref, *, act, transposed_w):
    _act_tile(x_ref, w_ref, o_ref, act, transposed_w)


def mm_act(x, w, *, col0, ncols, act, out_dtype, transposed_w=False):
    m, k = x.shape
    tm = _tile(m, ROW_TILE)
    tn = _col_tile(m, ncols)
    assert col0 % tn == 0
    j0 = col0 // tn
    if transposed_w:
        w_spec = pl.BlockSpec((tn, k), lambda i, j: (j + j0, 0))
    else:
        w_spec = pl.BlockSpec((k, tn), lambda i, j: (0, j + j0))
    return pl.pallas_call(
        functools.partial(_mm_act_kernel, act=act, transposed_w=transposed_w),
        grid=(m // tm, ncols // tn),
        in_specs=[pl.BlockSpec((tm, k), lambda i, j: (i, 0)), w_spec],
        out_specs=pl.BlockSpec((tm, tn), lambda i, j: (i, j)),
        out_shape=jax.ShapeDtypeStruct((m, ncols), out_dtype),
        compiler_params=_params(2),
        name="mm_act",
    )(x, w)


def _mm_gates_kernel(x_ref, w_ref, o_ref, *, per_gate):
    @pl.when(pl.program_id(1) < per_gate)
    def _():
        _act_tile(x_ref, w_ref, o_ref, jax.nn.gelu, True)

    @pl.when(pl.program_id(1) >= per_gate)
    def _():
        _act_tile(x_ref, w_ref, o_ref, _silu, True)


def mm_gates(x, w_in_t, row_starts):
    m, k = x.shape
    tm = _tile(m, ROW_TILE)
    tn = _tile(W_A, 512)
    per_gate = W_A // tn
    u0, za0, zb0 = row_starts

    unit = QK_ROPE
    assert all(r % unit == 0 for r in row_starts) and tn % unit == 0

    def w_map(i, j):
        start = jnp.where(j < per_gate, u0 // unit,
                          jnp.where(j < 2 * per_gate, (za0 - W_A) // unit, (zb0 - 2 * W_A) // unit))
        return ((start + j * (tn // unit)) * unit, 0)

    return pl.pallas_call(
        functools.partial(_mm_gates_kernel, per_gate=per_gate),
        grid=(m // tm, 3 * per_gate),
        in_specs=[pl.BlockSpec((tm, k), lambda i, j: (i, 0)),
                  pl.BlockSpec((pl.Element(tn), pl.Element(k)), w_map)],
        out_specs=pl.BlockSpec((tm, tn), lambda i, j: (i, j)),
        out_shape=jax.ShapeDtypeStruct((m, 3 * W_A), BF16),
        compiler_params=_params(2),
        name="mm_gates",
    )(x, w_in_t)


def _mm_plain_kernel(*refs, n_in):
    x_refs = refs[:n_in]
    w_refs = refs[n_in:2 * n_in]
    o_ref = refs[2 * n_in]
    acc = _bf16_dot(x_refs[0][...], w_refs[0][...])
    for x_ref, w_ref in zip(x_refs[1:], w_refs[1:]):
        acc += _bf16_dot(x_ref[...], w_ref[...])
    o_ref[...] = acc.astype(o_ref.dtype)


def mm_plain(xs, w, out_dtype):
    m = xs[0].shape[0]
    n = w.shape[1]
    tm = _tile(m, ROW_TILE)
    tn = _col_tile(m, n)
    in_specs = [pl.BlockSpec((tm, x.shape[1]), lambda i, j: (i, 0)) for x in xs]
    row = 0
    for x in xs:
        kx = x.shape[1]
        assert row % kx == 0
        in_specs.append(pl.BlockSpec((kx, tn), functools.partial(lambda i, j, r: (r, j), r=row // kx)))
        row += kx
    assert row == w.shape[0]
    return pl.pallas_call(
        functools.partial(_mm_plain_kernel, n_in=len(xs)),
        grid=(m // tm, n // tn),
        in_specs=in_specs,
        out_specs=pl.BlockSpec((tm, tn), lambda i, j: (i, j)),
        out_shape=jax.ShapeDtypeStruct((m, n), out_dtype),
        compiler_params=_params(2),
        name="mm_plain",
    )(*xs, *([w] * len(xs)))


def _mm_rms_kernel(x_ref, w_ref, g_ref, o_ref):
    acc = _nt_dot(x_ref[...], w_ref[...])
    o_ref[...] = _rms(acc, g_ref[...]).astype(o_ref.dtype)


def mm_rms(x, w_t, g):
    m, k = x.shape
    n = w_t.shape[0]
    tm = _tile(m, 1024)
    return pl.pallas_call(
        _mm_rms_kernel,
        grid=(m // tm,),
        in_specs=[pl.BlockSpec((tm, k), lambda i: (i, 0)), pl.BlockSpec((n, k), lambda i: (0, 0)),
                  pl.BlockSpec((1, n), lambda i: (0, 0))],
        out_specs=pl.BlockSpec((tm, n), lambda i: (i, 0)),
        out_shape=jax.ShapeDtypeStruct((m, n), BF16),
        compiler_params=_params(1),
        name="mm_rms",
    )(x, w_t, g.reshape(1, n))


def _rope_from_pair(pair, cs):
    t = pair * cs
    return (t + pltpu.roll(t, QK_ROPE, axis=1))[:, :QK_ROPE]


def _mm_kv_kernel(x_ref, w_ref, g_ref, cs_ref, c_ref, kpe_ref, kcat_ref):
    acc = _nt_dot(x_ref[...], w_ref[...])
    c = _rms(acc[:, :KV_RANK], g_ref[...])
    kpe = _rope_from_pair(acc[:, KV_RANK:], cs_ref[...])
    c_ref[...] = c
    kpe_ref[...] = kpe
    kcat_ref[:, :KV_RANK] = c.astype(BF16)
    kcat_ref[:, KV_RANK:] = kpe.astype(BF16)


def mm_kv(x, w_kv, g_kv, cs):
    m, k = x.shape
    n = w_kv.shape[0]
    tm = _tile(m, 1024)
    return pl.pallas_call(
        _mm_kv_kernel,
        grid=(m // tm,),
        in_specs=[pl.BlockSpec((tm, k), lambda i: (i, 0)), pl.BlockSpec((n, k), lambda i: (0, 0)),
                  pl.BlockSpec((1, KV_RANK), lambda i: (0, 0)), pl.BlockSpec((tm, LANES), lambda i: (i, 0))],
        out_specs=[pl.BlockSpec((tm, KV_RANK), lambda i: (i, 0)), pl.BlockSpec((tm, QK_ROPE), lambda i: (i, 0)),
                   pl.BlockSpec((tm, K_CAT), lambda i: (i, 0))],
        out_shape=[jax.ShapeDtypeStruct((m, KV_RANK), F32), jax.ShapeDtypeStruct((m, QK_ROPE), F32),
                   jax.ShapeDtypeStruct((m, K_CAT), BF16)],
        compiler_params=_params(1),
        name="mm_kv",
    )(x, w_kv, g_kv.reshape(1, KV_RANK), cs)


def _mm_q_kernel(x_ref, wq_ref, wn_ref, cs_ref, o_ref):
    x = x_ref[...]
    cs = cs_ref[...]
    groups = x.shape[0] // QBLOCK
    q_all = jnp.dot(x, wq_ref[...], preferred_element_type=F32)
    for h in range(HEADS):
        qh = q_all[:, h * 256:(h + 1) * 256]
        q_abs = jnp.dot(qh[:, :QK_NOPE].astype(BF16), wn_ref[h], preferred_element_type=F32) * ATTN_SCALE
        q_pe = _rope_from_pair(qh[:, QK_NOPE:], cs) * ATTN_SCALE
        o_ref[:, h, :, :KV_RANK] = q_abs.astype(BF16).reshape(groups, QBLOCK, KV_RANK)
        o_ref[:, h, :, KV_RANK:] = q_pe.astype(BF16).reshape(groups, QBLOCK, QK_ROPE)


def mm_q(qn, w_q, w_nope, cs):
    m, k = qn.shape
    tm = _tile(m, 512)
    return pl.pallas_call(
        _mm_q_kernel,
        grid=(m // tm,),
        in_specs=[pl.BlockSpec((tm, k), lambda i: (i, 0)),
                  pl.BlockSpec(w_q.shape, lambda i: (0, 0)),
                  pl.BlockSpec(w_nope.shape, lambda i: (0, 0, 0)),
                  pl.BlockSpec((tm, LANES), lambda i: (i, 0))],
        out_specs=pl.BlockSpec((tm // QBLOCK, HEADS, QBLOCK, K_CAT), lambda i: (i, 0, 0, 0)),
        out_shape=jax.ShapeDtypeStruct((m // QBLOCK, HEADS, QBLOCK, K_CAT), BF16),
        compiler_params=_params(1),
        name="mm_q",
    )(qn, w_q, w_nope, cs)


def _mm_glu_kernel(x_ref, wa_ref, wg_ref, o_ref):
    x = x_ref[...]
    a = _bf16_dot(x, wa_ref[...])
    b = _bf16_dot(x, wg_ref[...])
    o_ref[...] = a * _sigmoid(b)


def mm_glu(x, w, width):
    m, k = x.shape
    tm = _tile(m, ROW_TILE)
    tn = _tile(width, 256)
    nb = width // tn
    return pl.pallas_call(
        _mm_glu_kernel,
        grid=(m // tm, nb),
        in_specs=[pl.BlockSpec((tm, k), lambda i, j: (i, 0)),
                  pl.BlockSpec((k, tn), lambda i, j: (0, j)),
                  pl.BlockSpec((k, tn), lambda i, j: (0, j + nb))],
        out_specs=pl.BlockSpec((tm, tn), lambda i, j: (i, j)),
        out_shape=jax.ShapeDtypeStruct((m, width), F32),
        compiler_params=_params(2),
        name="mm_glu",
    )(x, w, w)


def _mix_a_kernel(gu_ref, gv_ref, sz_ref, lg_ref, lb_ref, wm_ref, bias_ref, oa_ref, vo_ref):
    vn = _layer_norm(gv_ref[...], lg_ref[...], lb_ref[...])
    vo_ref[...] = vn
    vb = vn.astype(BF16)
    for g in range(A_GROUPS):
        sl = slice(g * A_GROUP_DIM, (g + 1) * A_GROUP_DIM)
        mix = jnp.dot(wm_ref[g], vb[:, sl], preferred_element_type=F32) + bias_ref[:, sl]
        oa_ref[:, sl] = (gu_ref[:, sl].astype(F32) * mix * sz_ref[:, sl].astype(F32)).astype(BF16)


def _softmax_update(s, v, m_ref, l_ref, acc_ref):
    n_rep = s.shape[1] // LANES
    m_prev = m_ref[...]
    m_new = jnp.maximum(m_prev, jnp.max(s, axis=1, keepdims=True))
    alpha = jnp.exp(m_prev - m_new)
    p = jnp.exp(s - jnp.concatenate([m_new] * n_rep, axis=1))
    l_ref[...] = alpha * l_ref[...] + jnp.sum(p, axis=1, keepdims=True)
    m_ref[...] = m_new
    acc_ref[...] = (acc_ref[...] * jnp.concatenate([alpha] * (KV_RANK // LANES), axis=1)
                    + jnp.dot(p.astype(BF16), v, preferred_element_type=F32))


def _softmax_init(m_ref, l_ref, acc_ref):
    m_ref[...] = jnp.full(m_ref.shape, NEG_INF, F32)
    l_ref[...] = jnp.zeros(l_ref.shape, F32)
    acc_ref[...] = jnp.zeros(acc_ref.shape, F32)


def _softmax_result(l_ref, acc_ref):
    inv = 1.0 / l_ref[...]
    return acc_ref[...] * jnp.concatenate([inv] * (KV_RANK // LANES), axis=1)


def _nt_dot(a, b):
    return lax.dot_general(a, b, (((1,), (1,)), ((), ())), preferred_element_type=F32)


def _prefill_kernel(q_ref, k_ref, sz_ref, wv_ref, o_ref, m_ref, l_ref, acc_ref, *, kc):
    qb = pl.program_id(1)
    rows = HEADS * QBLOCK
    q = q_ref[0].reshape(rows, K_CAT)
    _softmax_init(m_ref, l_ref, acc_ref)

    def chunk(k0, width, masked):
        k = k_ref[0, pl.ds(k0, width), :]
        s = _nt_dot(q, k)
        if masked:
            q_pos = qb * QBLOCK + (lax.broadcasted_iota(jnp.int32, (rows, width), 0) & (QBLOCK - 1))
            k_pos = k0 + lax.broadcasted_iota(jnp.int32, (rows, width), 1)
            s = jnp.where(k_pos <= q_pos, s, NEG_INF)
        _softmax_update(s, k[:, :KV_RANK], m_ref, l_ref, acc_ref)

    n_full = (qb * QBLOCK) // kc

    def body(i, carry):
        chunk(pl.multiple_of(i * kc, kc), kc, False)
        return carry

    lax.fori_loop(0, n_full, body, 0)
    k_diag = pl.multiple_of(n_full * kc, kc)
    blocks_in = qb - n_full * (kc // QBLOCK)
    for w in range(kc // QBLOCK):
        @pl.when(blocks_in == w)
        def _():
            chunk(k_diag, (w + 1) * QBLOCK, True)

    o = _softmax_result(l_ref, acc_ref)
    for h in range(HEADS):
        sl = slice(h * V_DIM, (h + 1) * V_DIM)
        r = jnp.dot(o[h * QBLOCK:(h + 1) * QBLOCK].astype(BF16), wv_ref[h], preferred_element_type=F32)
        o_ref[0, :, sl] = (r * sz_ref[0, :, sl].astype(F32)).astype(BF16)


def mla_prefill(q, kcat, sz, w_v, zb_col_block):
    b, l, _ = kcat.shape
    nq = l // QBLOCK
    kc = _tile(l, 512)
    rows = HEADS * QBLOCK
    return pl.pallas_call(
        functools.partial(_prefill_kernel, kc=kc),
        grid=(b, nq),
        in_specs=[pl.BlockSpec((1, HEADS, QBLOCK, K_CAT), lambda i, j: (i * nq + j, 0, 0, 0)),
                  pl.BlockSpec((1, l, K_CAT), lambda i, j: (i, 0, 0)),
                  pl.BlockSpec((1, QBLOCK, W_B), lambda i, j: (i, j, zb_col_block)),
                  pl.BlockSpec(w_v.shape, lambda i, j: (0, 0, 0))],
        out_specs=pl.BlockSpec((1, QBLOCK, W_B), lambda i, j: (i, j, 0)),
        out_shape=jax.ShapeDtypeStruct((b, l, W_B), BF16),
        scratch_shapes=[pltpu.VMEM((rows, LANES), F32), pltpu.VMEM((rows, LANES), F32),
                        pltpu.VMEM((rows, KV_RANK), F32)],
        compiler_params=_params(2),
        name="mla_prefill",
    )(q, kcat, sz, w_v)


def _decode_kernel(pt_ref, q_ref, knew_ref, cc_hbm, cr_hbm, o_ref,
                   cbuf, rbuf, cb_ref, rb_ref, m_ref, l_ref, acc_ref, sem_c, sem_r,
                   *, layer, n_pages, ch, lq):
    b = pl.program_id(0)
    n_chunks = n_pages // ch
    total = pl.num_programs(0) * n_chunks
    ahead = DECODE_SLOTS - 1

    def chunk_copies(g):
        slot = lax.rem(g, DECODE_SLOTS)
        copies = []
        for p in range(ch):
            page = pt_ref[g * ch + p]
            copies.append(pltpu.make_async_copy(cc_hbm.at[layer, page], cbuf.at[slot, pl.ds(p * PAGE, PAGE)],
                                                sem_c.at[slot]))
            copies.append(pltpu.make_async_copy(cr_hbm.at[layer, page], rbuf.at[slot, p], sem_r.at[slot]))
        return copies

    def start_chunk(g):
        for n, cp in enumerate(chunk_copies(g)):
            cp.start(priority=(n // 2) % 2)

    @pl.when(b == 0)
    def _():
        for g in range(ahead):
            start_chunk(g)

    _softmax_init(m_ref, l_ref, acc_ref)
    q = q_ref[0]

    def chunk_body(j, carry):
        g = b * n_chunks + j

        @pl.when(g + ahead < total)
        def _():
            start_chunk(g + ahead)

        for cp in chunk_copies(g):
            cp.wait()
        slot = lax.rem(g, DECODE_SLOTS)
        cb_ref[...] = cbuf[slot].astype(BF16)
        for p in range(ch):
            rb_ref[:, p * PAGE:(p + 1) * PAGE] = rbuf[slot, p].astype(BF16)
        cb = cb_ref[...]
        s = _nt_dot(q[:, :KV_RANK], cb) + jnp.dot(q[:, KV_RANK:], rb_ref[...], preferred_element_type=F32)
        _softmax_update(s, cb, m_ref, l_ref, acc_ref)
        return carry

    lax.fori_loop(0, n_chunks, chunk_body, 0)

    kn = knew_ref[0]
    rows = q.shape[0]
    s_new = _nt_dot(q, kn)
    q_pos = lax.broadcasted_iota(jnp.int32, (rows, PAGE), 0) & (lq - 1)
    k_pos = lax.broadcasted_iota(jnp.int32, (rows, PAGE), 1)
    s_new = jnp.where(k_pos <= q_pos, s_new, NEG_INF)
    _softmax_update(s_new, kn[:, :KV_RANK], m_ref, l_ref, acc_ref)
    o_ref[0] = _softmax_result(l_ref, acc_ref).astype(BF16)


def mla_decode(q, knew, cache_c, cache_rt, layer, page_table):
    bd, rows, _ = q.shape
    lq = rows // HEADS
    assert lq & (lq - 1) == 0
    n_pages = page_table.shape[1]
    ch = DECODE_PAGES_PER_CHUNK if n_pages % DECODE_PAGES_PER_CHUNK == 0 else n_pages
    assert bd * (n_pages // ch) >= DECODE_SLOTS
    seq_block = lambda b, pt: (b, 0, 0)
    grid_spec = pltpu.PrefetchScalarGridSpec(
        num_scalar_prefetch=1,
        grid=(bd,),
        in_specs=[pl.BlockSpec((1, rows, K_CAT), seq_block),
                  pl.BlockSpec((1, PAGE, K_CAT), seq_block),
                  pl.BlockSpec(memory_space=pl.ANY),
                  pl.BlockSpec(memory_space=pl.ANY)],
        out_specs=pl.BlockSpec((1, rows, KV_RANK), seq_block),
        scratch_shapes=[pltpu.VMEM((DECODE_SLOTS, ch * PAGE, KV_RANK), F32),
                        pltpu.VMEM((DECODE_SLOTS, ch, QK_ROPE, PAGE), F32),
                        pltpu.VMEM((ch * PAGE, KV_RANK), BF16), pltpu.VMEM((QK_ROPE, ch * PAGE), BF16),
                        pltpu.VMEM((rows, LANES), F32), pltpu.VMEM((rows, LANES), F32),
                        pltpu.VMEM((rows, KV_RANK), F32),
                        pltpu.SemaphoreType.DMA((DECODE_SLOTS,)), pltpu.SemaphoreType.DMA((DECODE_SLOTS,))],
    )
    return pl.pallas_call(
        functools.partial(_decode_kernel, layer=layer, n_pages=n_pages, ch=ch, lq=lq),
        grid_spec=grid_spec,
        out_shape=jax.ShapeDtypeStruct((bd, rows, KV_RANK), BF16),
        compiler_params=_params_seq(("arbitrary",)),
        name="mla_decode",
    )(page_table.reshape(-1), q, knew, cache_c, cache_rt)


def _v_up_kernel(o_ref, wv_ref, sz_ref, out_ref):
    r = jnp.dot(o_ref[0], wv_ref[0], preferred_element_type=F32)
    out_ref[...] = (r * sz_ref[...].astype(F32)).astype(BF16)


def v_up(o_heads, w_v, sz, zb_col0):
    _, m, _ = o_heads.shape
    jb = zb_col0 // V_DIM
    return pl.pallas_call(
        _v_up_kernel,
        grid=(HEADS,),
        in_specs=[pl.BlockSpec((1, m, KV_RANK), lambda h: (h, 0, 0)),
                  pl.BlockSpec((1, KV_RANK, V_DIM), lambda h: (h, 0, 0)),
                  pl.BlockSpec((m, V_DIM), lambda h: (0, h + jb))],
        out_specs=pl.BlockSpec((m, V_DIM), lambda h: (0, h)),
        out_shape=jax.ShapeDtypeStruct((m, W_B), BF16),
        compiler_params=_params(1),
        name="v_up",
    )(o_heads, w_v, sz)


def _conv_kernel(prev_ref, g_ref, sz_ref, w_ref, b_ref, lg_ref, lb_ref, o_ref, xp_ref, xs_ref, d_ref,
                 *, tl, lane_chunk):
    c = g_ref.shape[-1]
    xp_ref[:CONV_HALO, :] = jnp.where(pl.program_id(1) == 0, 0.0, prev_ref[0])
    xp_ref[CONV_HALO:CONV_HALO + tl, :] = g_ref[0]
    base = CONV_HALO - (CONV_W - 1)

    def lane_body(ci, carry):
        c0 = pl.multiple_of(ci * lane_chunk, lane_chunk)
        lanes = pl.ds(c0, lane_chunk)
        for phase in range(1, SUBLANES):
            xs_ref[phase] = xp_ref[phase:phase + tl + CONV_HALO - SUBLANES, lanes]
        acc = jnp.broadcast_to(b_ref[:, lanes], (tl, lane_chunk))
        for k in range(CONV_W):
            phase = (base + k) % SUBLANES
            off = (base + k) - phase
            if phase == 0:
                x = xp_ref[off:off + tl, lanes]
            else:
                x = xs_ref[phase, off:off + tl, :]
            acc = acc + w_ref[k:k + 1, lanes] * x
        d_ref[:, lanes] = acc
        return carry

    lax.fori_loop(0, c // lane_chunk, lane_body, 0)

    rows = min(tl, LN_ROWS)

    def norm_body(ri, carry):
        r = pl.ds(pl.multiple_of(ri * rows, rows), rows)
        y = _layer_norm(d_ref[r, :], lg_ref[...], lb_ref[...])
        o_ref[0, r, :] = (_silu(y) * sz_ref[0, r, :].astype(F32)).astype(BF16)
        return carry

    lax.fori_loop(0, tl // rows, norm_body, 0)


def conv_module(g, sz, w_dw, b_dw, ln_g, ln_b):
    b, l, c = g.shape
    tl = _tile(l, CONV_TILE)
    per = tl // CONV_HALO
    vec = lambda i, j: (0, 0)
    lane_chunk = LANES
    return pl.pallas_call(
        functools.partial(_conv_kernel, tl=tl, lane_chunk=lane_chunk),
        grid=(b, l // tl),
        in_specs=[pl.BlockSpec((1, CONV_HALO, c), lambda i, j: (i, jnp.maximum(j * per - 1, 0), 0)),
                  pl.BlockSpec((1, tl, c), lambda i, j: (i, j, 0)),
                  pl.BlockSpec((1, tl, c), lambda i, j: (i, j, 0)),
                  pl.BlockSpec((CONV_W, c), vec), pl.BlockSpec((1, c), vec),
                  pl.BlockSpec((1, c), vec), pl.BlockSpec((1, c), vec)],
        out_specs=pl.BlockSpec((1, tl, c), lambda i, j: (i, j, 0)),
        out_shape=jax.ShapeDtypeStruct((b, l, c), BF16),
        scratch_shapes=[pltpu.VMEM((CONV_HALO + tl, c), F32),
                        pltpu.VMEM((SUBLANES, tl + CONV_HALO - SUBLANES, lane_chunk), F32),
                        pltpu.VMEM((tl, c), F32)],
        compiler_params=_params(2),
        name="conv_module",
    )(g, g, sz, w_dw, b_dw.reshape(1, c), ln_g.reshape(1, c), ln_b.reshape(1, c))


def _conv_step_kernel(st_ref, g_ref, sz_ref, w_ref, b_ref, lg_ref, lb_ref, o_ref, ns_ref, d_ref, *, lane_chunk):
    n_prev = st_ref.shape[0]
    lq, bb, c = g_ref.shape

    def row(j, lanes):
        return st_ref[j, :, lanes] if j < n_prev else g_ref[j - n_prev, :, lanes]

    def lane_body(ci, carry):
        lanes = pl.ds(pl.multiple_of(ci * lane_chunk, lane_chunk), lane_chunk)
        acc = [jnp.broadcast_to(b_ref[:, lanes], (bb, lane_chunk)) for _ in range(lq)]
        for j in range(n_prev - (CONV_W - 1), n_prev + lq):
            x = row(j, lanes)
            for t in range(lq):
                k = j - (n_prev - (CONV_W - 1)) - t
                if 0 <= k < CONV_W:
                    acc[t] = acc[t] + w_ref[k:k + 1, lanes] * x
        for t in range(lq):
            d_ref[t, :, lanes] = acc[t]
        return carry

    lax.fori_loop(0, c // lane_chunk, lane_body, 0)
    for t in range(lq):
        y = _layer_norm(d_ref[t], lg_ref[...], lb_ref[...])
        o_ref[t] = (_silu(y) * sz_ref[t].astype(F32)).astype(BF16)
    ns_ref[:n_prev - lq] = st_ref[lq:]
    ns_ref[n_prev - lq:] = g_ref[...]


def conv_step(state_t, g_t, sz_t, w_dw, b_dw, ln_g, ln_b):
    n_prev, b, c = state_t.shape
    lq = g_t.shape[0]
    assert n_prev == CONV_W - 1 and lq < n_prev
    bb = _tile(b, 16)
    vec = lambda i: (0, 0)
    blk = lambda i: (0, i, 0)
    return pl.pallas_call(
        functools.partial(_conv_step_kernel, lane_chunk=min(c, 4 * LANES)),
        grid=(b // bb,),
        in_specs=[pl.BlockSpec((n_prev, bb, c), blk), pl.BlockSpec((lq, bb, c), blk), pl.BlockSpec((lq, bb, c), blk),
                  pl.BlockSpec((CONV_W, c), vec), pl.BlockSpec((1, c), vec),
                  pl.BlockSpec((1, c), vec), pl.BlockSpec((1, c), vec)],
        out_specs=[pl.BlockSpec((lq, bb, c), blk), pl.BlockSpec((n_prev, bb, c), blk)],
        out_shape=[jax.ShapeDtypeStruct((lq, b, c), BF16), jax.ShapeDtypeStruct((n_prev, b, c), F32)],
        scratch_shapes=[pltpu.VMEM((lq, bb, c), F32)],
        compiler_params=_params(1),
        name="conv_step",
    )(state_t, g_t, sz_t, w_dw, b_dw.reshape(1, c), ln_g.reshape(1, c), ln_b.reshape(1, c))


def _rope_cs(length, offset, reps):
    inv = jnp.power(ROPE_THETA, -jnp.arange(0, QK_ROPE, 2, dtype=F32) / QK_ROPE)
    pos = jnp.arange(length, dtype=F32) + offset
    ang = pos[:, None] * inv[None, :]
    cos, sin = jnp.cos(ang), jnp.sin(ang)
    return jnp.tile(jnp.concatenate([cos, cos, -sin, sin], axis=-1), (reps, 1))


def _swap_halves(w):
    half = w.shape[-1] // 2
    return jnp.concatenate([w[..., half:], w[..., :half]], axis=-1)


def _even_weights(w_in, w_q_b, w_kv_b):
    wt = w_in.T
    o_qa = 3 * W_A
    o_kv = o_qa + Q_RANK
    o_kr = o_kv + KV_RANK
    o_zb = o_kr + QK_ROPE
    w_mid = lax.optimization_barrier(wt[o_qa:o_zb]).astype(BF16)
    w_qa = w_mid[:Q_RANK]
    w_kr = w_mid[Q_RANK + KV_RANK:]
    w_kr_sw = jnp.concatenate([w_kr[QK_ROPE // 2:], w_kr[:QK_ROPE // 2]], axis=0)
    w_kv = jnp.concatenate([w_mid[Q_RANK:], w_kr_sw], axis=0)
    wq = w_q_b.reshape(Q_RANK, HEADS, QK_NOPE + QK_ROPE)
    wq_rope = wq[..., QK_NOPE:]
    w_q = jnp.concatenate([wq[..., :QK_NOPE], wq_rope, _swap_halves(wq_rope)], axis=-1)
    w_q = w_q.reshape(Q_RANK, HEADS * 256).astype(BF16)
    wkvb = w_kv_b.reshape(KV_RANK, HEADS, QK_NOPE + V_DIM)
    w_nope = jnp.transpose(wkvb[..., :QK_NOPE], (1, 2, 0)).astype(BF16)
    w_vup = jnp.transpose(wkvb[..., QK_NOPE:], (1, 0, 2)).astype(BF16)
    return wt, w_qa, w_kv, w_q, w_nope, w_vup


def _spatial_weights(w_s, b_s, seq_len):
    cl = min(CHUNK, seq_len)
    reps = CHUNK // cl
    wm = (w_s * jnp.tril(jnp.ones((CHUNK, CHUNK), w_s.dtype)))[:, :cl, :cl]
    eye = jnp.eye(reps, dtype=w_s.dtype)
    wm = jnp.einsum("ab,gts->gatbs", eye, wm).reshape(A_GROUPS, CHUNK, CHUNK).astype(BF16)
    bias = jnp.tile(b_s[:, :cl].T, (reps, 1))
    bias = jnp.repeat(bias, A_GROUP_DIM, axis=1)
    return wm, bias


def _even_layer(x, h, batch, seq_len, offset, ew, w_out, ln_v_g, ln_v_b, w_s, b_s, g_q_a, g_kv_a, g_post, g_next,
                past):
    w_in_t, w_qa, w_kv, w_q, w_nope, w_vup = ew
    m = x.shape[0]
    z_b_row0 = 3 * W_A + Q_RANK + KV_RANK + QK_ROPE
    gates = mm_gates(h, w_in_t, (0, 2 * W_A, z_b_row0))
    gv = mm_act(h, w_in_t, col0=W_A, ncols=W_A, act=jax.nn.gelu, out_dtype=F32, transposed_w=True)
    qn = mm_rms(h, w_qa, g_q_a)
    cs = _rope_cs(seq_len, offset, batch)
    c, k_pe, kcat = mm_kv(h, w_kv, g_kv_a, cs)

    wm, bias = _spatial_weights(w_s, b_s, seq_len)
    blocks_per_seq = max(seq_len // CHUNK, 1)
    out_a, v_open = _mix_a_call(gates, gv, ln_v_g, ln_v_b, wm, bias, blocks_per_seq)

    q = mm_q(qn, w_q, w_nope, cs)
    if past is None:
        out_b = mla_prefill(q, kcat.reshape(batch, seq_len, K_CAT), gates.reshape(batch, seq_len, 3 * W_A),
                            w_vup, zb_col_block=2).reshape(m, W_B)
    else:
        cache_c, cache_rt, layer, page_table = past
        per = QBLOCK // seq_len
        qd = q.reshape(m // QBLOCK, HEADS, per, seq_len, K_CAT)
        qd = jnp.transpose(qd, (0, 2, 1, 3, 4)).reshape(batch, HEADS * seq_len, K_CAT)
        knew = jnp.pad(kcat.reshape(batch, seq_len, K_CAT), ((0, 0), (0, PAGE - seq_len), (0, 0)))
        o_lat = mla_decode(qd, knew, cache_c, cache_rt, layer, page_table)
        o_heads = jnp.transpose(o_lat.reshape(batch, HEADS, seq_len, KV_RANK), (1, 0, 2, 3))
        out_b = v_up(o_heads.reshape(HEADS, m, KV_RANK), w_vup, gates, zb_col0=2 * W_A)
    y = mm_plain([out_a, out_b], w_out, BF16)
    x_new, h_next = residual_norm(x, y, g_post, g_next)
    return x_new, h_next, c, k_pe, v_open


def _mix_a_call(gates, gv, ln_v_g, ln_v_b, wm, bias, blocks_per_seq):
    m = gv.shape[0]
    nblk = m // CHUNK
    row0 = lambda i: (i, 0)
    row1 = lambda i: (i, 1)
    fixed2 = lambda i: (0, 0)
    return pl.pallas_call(
        _mix_a_kernel,
        grid=(nblk,),
        in_specs=[pl.BlockSpec((CHUNK, W_A), row0), pl.BlockSpec((CHUNK, W_A), row0),
                  pl.BlockSpec((CHUNK, W_A), row1),
                  pl.BlockSpec((1, W_A), fixed2), pl.BlockSpec((1, W_A), fixed2),
                  pl.BlockSpec((A_GROUPS, CHUNK, CHUNK), lambda i: (0, 0, 0)),
                  pl.BlockSpec((CHUNK, W_A), fixed2)],
        out_specs=[pl.BlockSpec((CHUNK, W_A), row0),
                   pl.BlockSpec((CHUNK, W_A), lambda i: (i // blocks_per_seq, 0))],
        out_shape=[jax.ShapeDtypeStruct((m, W_A), BF16),
                   jax.ShapeDtypeStruct((m // blocks_per_seq, W_A), F32)],
        compiler_params=_params_seq(("arbitrary",)),
        name="mix_a",
    )(gates, gv, gates, ln_v_g.reshape(1, W_A), ln_v_b.reshape(1, W_A), wm, bias)


def _odd_layer(x, h, batch, seq_len, conv_prev, w_in, w_dw, b_dw, ln_c_g, ln_c_b, w_out, g_post, g_next):
    m = x.shape[0]
    c = w_dw.shape[1]
    g = mm_glu(h, w_in, c).reshape(batch, seq_len, c)
    sz = mm_act(h, w_in, col0=2 * c, ncols=c, act=_silu, out_dtype=BF16).reshape(batch, seq_len, c)
    if conv_prev is None:
        y2 = conv_module(g, sz, w_dw, b_dw, ln_c_g, ln_c_b)
        new_state = g[:, seq_len - (CONV_W - 1):]
    else:
        tm = lambda a: jnp.swapaxes(a, 0, 1)
        y2, new_state = conv_step(tm(conv_prev), tm(g), tm(sz), w_dw, b_dw, ln_c_g, ln_c_b)
        y2, new_state = tm(y2), tm(new_state)
    y = mm_plain([y2.reshape(m, c)], w_out, BF16)
    x_new, h_next = residual_norm(x, y, g_post, g_next)
    return x_new, h_next, new_state


def kernel(x_prompt, x_sample, cache_kv_latent, cache_k_rope, state_conv, page_table, w_in_even, ln_v_g, ln_v_b,
           w_s, b_s, g_q_a, w_q_b, g_kv_a, w_kv_b, w_out_even, w_in_odd, w_dw, b_dw, ln_c_g, ln_c_b, w_out_odd,
           g_pre, g_post):
    bp, lp, d = x_prompt.shape
    bs, ls, _ = x_sample.shape
    past_len = page_table.shape[1] * PAGE
    depth = g_pre.shape[0]
    xp = x_prompt.reshape(bp * lp, d)
    xs = x_sample.reshape(bs * ls, d)
    lat_p, kpe_p, v_p, conv_p = [], [], [], []
    lat_s, kpe_s, v_s, conv_s = [], [], [], []
    hp = rmsnorm_cast(xp, g_pre[0])
    hs = rmsnorm_cast(xs, g_pre[0])
    for i in range(depth):
        j = i // 2
        g_next = g_pre[i + 1] if i + 1 < depth else None
        if i % 2 == 0:
            ew = _even_weights(w_in_even[j], w_q_b[j], w_kv_b[j])
            args = (ew, w_out_even[j], ln_v_g[j], ln_v_b[j], w_s[j], b_s[j], g_q_a[j], g_kv_a[j], g_post[i], g_next)
            xp, hp, c, k, v = _even_layer(xp, hp, bp, lp, 0, *args, None)
            lat_p.append(c.reshape(bp, lp, KV_RANK))
            kpe_p.append(k.reshape(bp, lp, QK_ROPE))
            v_p.append(v.reshape(bp, -1, W_A))
            cache_rt = jnp.swapaxes(cache_k_rope, 2, 3)
            xs, hs, c, k, v = _even_layer(xs, hs, bs, ls, past_len, *args,
                                          (cache_kv_latent, cache_rt, j, page_table))
            lat_s.append(c.reshape(bs, ls, KV_RANK))
            kpe_s.append(k.reshape(bs, ls, QK_ROPE))
            v_s.append(v.reshape(bs, ls, W_A))
        else:
            args = (w_in_odd[j], w_dw[j], b_dw[j], ln_c_g[j], ln_c_b[j], w_out_odd[j], g_post[i], g_next)
            xp, hp, new_state = _odd_layer(xp, hp, bp, lp, None, *args)
            conv_p.append(new_state)
            xs, hs, new_state = _odd_layer(xs, hs, bs, ls, state_conv[j], *args)
            conv_s.append(new_state)
    return (xp.reshape(bp, lp, d), xs.reshape(bs, ls, d), jnp.stack(lat_p), jnp.stack(kpe_p), jnp.stack(v_p),
            jnp.stack(conv_p), jnp.stack(lat_s), jnp.stack(kpe_s), jnp.stack(v_s), jnp.stack(conv_s))
```

```python
import functools
import math

import jax
import jax.numpy as jnp
from jax import lax
from jax.experimental import pallas as pl
from jax.experimental.pallas import tpu as pltpu

F32 = jnp.float32
BF16 = jnp.bfloat16

EPS = 1e-6
LANES = 128
SUBLANES = 8
CHUNK = 128
A_GROUPS = 16
A_GROUP_DIM = 128
W_A = A_GROUPS * A_GROUP_DIM
HEADS = 16
QK_NOPE = 128
QK_ROPE = 64
V_DIM = 128
Q_RANK = 1024
KV_RANK = 512
W_B = HEADS * V_DIM
K_CAT = KV_RANK + QK_ROPE
ROPE_THETA = 10000.0
QBLOCK = 128
PAGE = 128
DECODE_PAGES_PER_CHUNK = 16
DECODE_SLOTS = 4
CONV_W = 31
CONV_HALO = 32
CONV_TILE = 256
LN_ROWS = 64
ATTN_SCALE = 1.0 / math.sqrt(QK_NOPE + QK_ROPE)
NEG_INF = float("-inf")
VMEM_LIMIT_BYTES = 56 * 1024 * 1024


def _params(n_axes):
    return pltpu.CompilerParams(dimension_semantics=("parallel",) * n_axes,
                                vmem_limit_bytes=VMEM_LIMIT_BYTES)


def _params_seq(sem):
    return pltpu.CompilerParams(dimension_semantics=sem, vmem_limit_bytes=VMEM_LIMIT_BYTES)


def _sigmoid(x):
    return 1.0 / (1.0 + jnp.exp(-x))


def _silu(x):
    return x * _sigmoid(x)


def _rms(x, g):
    return x * lax.rsqrt(jnp.mean(x * x, axis=-1, keepdims=True) + EPS) * g


def _layer_norm(x, g, b):
    mu = jnp.mean(x, axis=-1, keepdims=True)
    xc = x - mu
    return xc * lax.rsqrt(jnp.mean(xc * xc, axis=-1, keepdims=True) + EPS) * g + b


def _tile(n, pref):
    return pref if n % pref == 0 else n


def _rmsnorm_kernel(x_ref, g_ref, o_ref):
    o_ref[...] = _rms(x_ref[...], g_ref[...]).astype(o_ref.dtype)


def rmsnorm_cast(x, g):
    m, d = x.shape
    tm = _tile(m, 256)
    return pl.pallas_call(
        _rmsnorm_kernel,
        grid=(m // tm,),
        in_specs=[pl.BlockSpec((tm, d), lambda i: (i, 0)), pl.BlockSpec((1, d), lambda i: (0, 0))],
        out_specs=pl.BlockSpec((tm, d), lambda i: (i, 0)),
        out_shape=jax.ShapeDtypeStruct((m, d), BF16),
        compiler_params=_params(1),
        name="rmsnorm_cast",
    )(x, g.reshape(1, d))


def _residual_norm_kernel(x_ref, y_ref, g_ref, o_ref):
    o_ref[...] = x_ref[...] + _rms(y_ref[...].astype(F32), g_ref[...])


def _residual_norm_next_kernel(x_ref, y_ref, g_ref, gn_ref, o_ref, h_ref):
    x_new = x_ref[...] + _rms(y_ref[...].astype(F32), g_ref[...])
    o_ref[...] = x_new
    h_ref[...] = _rms(x_new, gn_ref[...]).astype(h_ref.dtype)


def residual_norm(x, y, g, g_next=None):
    m, d = x.shape
    tm = _tile(m, 256)
    row = pl.BlockSpec((tm, d), lambda i: (i, 0))
    vec = pl.BlockSpec((1, d), lambda i: (0, 0))
    if g_next is None:
        return pl.pallas_call(
            _residual_norm_kernel,
            grid=(m // tm,),
            in_specs=[row, row, vec],
            out_specs=row,
            out_shape=jax.ShapeDtypeStruct((m, d), F32),
            compiler_params=_params(1),
            name="residual_norm",
        )(x, y, g.reshape(1, d)), None
    return pl.pallas_call(
        _residual_norm_next_kernel,
        grid=(m // tm,),
        in_specs=[row, row, vec, vec],
        out_specs=[row, row],
        out_shape=[jax.ShapeDtypeStruct((m, d), F32), jax.ShapeDtypeStruct((m, d), BF16)],
        compiler_params=_params(1),
        name="residual_norm_next",
    )(x, y, g.reshape(1, d), g_next.reshape(1, d))


ROW_TILE = 1024


def _col_tile(m, n):
    return _tile(n, 512 if m >= ROW_TILE else 1024)


def _bf16_dot(x, w, transposed_w=False):
    w = w.astype(BF16)
    if transposed_w:
        return _nt_dot(x, w)
    return jnp.dot(x, w, preferred_element_type=F32)


def _act_tile(x_ref, w_ref, o_ref, act, transposed_w):
    o_ref[...] = act(_bf16_dot(x_ref[...], w_ref[...], transposed_w)).astype(o_ref.dtype)


def _mm_act_kernel(x_ref, w_ref, o_ref, *, act, transposed_w):
    _act_tile(x_ref, w_ref, o_ref, act, transposed_w)


def mm_act(x, w, *, col0, ncols, act, out_dtype, transposed_w=False):
    m, k = x.shape
    tm = _tile(m, ROW_TILE)
    tn = _col_tile(m, ncols)
    assert col0 % tn == 0
    j0 = col0 // tn
    if transposed_w:
        w_spec = pl.BlockSpec((tn, k), lambda i, j: (j + j0, 0))
    else:
        w_spec = pl.BlockSpec((k, tn), lambda i, j: (0, j + j0))
    return pl.pallas_call(
        functools.partial(_mm_act_kernel, act=act, transposed_w=transposed_w),
        grid=(m // tm, ncols // tn),
        in_specs=[pl.BlockSpec((tm, k), lambda i, j: (i, 0)), w_spec],
        out_specs=pl.BlockSpec((tm, tn), lambda i, j: (i, j)),
        out_shape=jax.ShapeDtypeStruct((m, ncols), out_dtype),
        compiler_params=_params(2),
        name="mm_act",
    )(x, w)


def _mm_gates_kernel(x_ref, w_ref, o_ref, *, per_gate):
    @pl.when(pl.program_id(1) < per_gate)
    def _():
        _act_tile(x_ref, w_ref, o_ref, jax.nn.gelu, True)

    @pl.when(pl.program_id(1) >= per_gate)
    def _():
        _act_tile(x_ref, w_ref, o_ref, _silu, True)


def mm_gates(x, w_in_t, row_starts):
    m, k = x.shape
    tm = _tile(m, ROW_TILE)
    tn = _tile(W_A, 512)
    per_gate = W_A // tn
    u0, za0, zb0 = row_starts

    unit = QK_ROPE
    assert all(r % unit == 0 for r in row_starts) and tn % unit == 0

    def w_map(i, j):
        start = jnp.where(j < per_gate, u0 // unit,
                          jnp.where(j < 2 * per_gate, (za0 - W_A) // unit, (zb0 - 2 * W_A) // unit))
        return ((start + j * (tn // unit)) * unit, 0)

    return pl.pallas_call(
        functools.partial(_mm_gates_kernel, per_gate=per_gate),
        grid=(m // tm, 3 * per_gate),
        in_specs=[pl.BlockSpec((tm, k), lambda i, j: (i, 0)),
                  pl.BlockSpec((pl.Element(tn), pl.Element(k)), w_map)],
        out_specs=pl.BlockSpec((tm, tn), lambda i, j: (i, j)),
        out_shape=jax.ShapeDtypeStruct((m, 3 * W_A), BF16),
        compiler_params=_params(2),
        name="mm_gates",
    )(x, w_in_t)


def _mm_plain_kernel(*refs, n_in):
    x_refs = refs[:n_in]
    w_refs = refs[n_in:2 * n_in]
    o_ref = refs[2 * n_in]
    acc = _bf16_dot(x_refs[0][...], w_refs[0][...])
    for x_ref, w_ref in zip(x_refs[1:], w_refs[1:]):
        acc += _bf16_dot(x_ref[...], w_ref[...])
    o_ref[...] = acc.astype(o_ref.dtype)


def mm_plain(xs, w, out_dtype):
    m = xs[0].shape[0]
    n = w.shape[1]
    tm = _tile(m, ROW_TILE)
    tn = _col_tile(m, n)
    in_specs = [pl.BlockSpec((tm, x.shape[1]), lambda i, j: (i, 0)) for x in xs]
    row = 0
    for x in xs:
        kx = x.shape[1]
        assert row % kx == 0
        in_specs.append(pl.BlockSpec((kx, tn), functools.partial(lambda i, j, r: (r, j), r=row // kx)))
        row += kx
    assert row == w.shape[0]
    return pl.pallas_call(
        functools.partial(_mm_plain_kernel, n_in=len(xs)),
        grid=(m // tm, n // tn),
        in_specs=in_specs,
        out_specs=pl.BlockSpec((tm, tn), lambda i, j: (i, j)),
        out_shape=jax.ShapeDtypeStruct((m, n), out_dtype),
        compiler_params=_params(2),
        name="mm_plain",
    )(*xs, *([w] * len(xs)))


def _mm_rms_kernel(x_ref, w_ref, g_ref, o_ref):
    acc = _nt_dot(x_ref[...], w_ref[...])
    o_ref[...] = _rms(acc, g_ref[...]).astype(o_ref.dtype)


def mm_rms(x, w_t, g):
    m, k = x.shape
    n = w_t.shape[0]
    tm = _tile(m, 1024)
    return pl.pallas_call(
        _mm_rms_kernel,
        grid=(m // tm,),
        in_specs=[pl.BlockSpec((tm, k), lambda i: (i, 0)), pl.BlockSpec((n, k), lambda i: (0, 0)),
                  pl.BlockSpec((1, n), lambda i: (0, 0))],
        out_specs=pl.BlockSpec((tm, n), lambda i: (i, 0)),
        out_shape=jax.ShapeDtypeStruct((m, n), BF16),
        compiler_params=_params(1),
        name="mm_rms",
    )(x, w_t, g.reshape(1, n))


def _rope_from_pair(pair, cs):
    t = pair * cs
    return (t + pltpu.roll(t, QK_ROPE, axis=1))[:, :QK_ROPE]


def _mm_kv_kernel(x_ref, w_ref, g_ref, cs_ref, c_ref, kpe_ref, kcat_ref):
    acc = _nt_dot(x_ref[...], w_ref[...])
    c = _rms(acc[:, :KV_RANK], g_ref[...])
    kpe = _rope_from_pair(acc[:, KV_RANK:], cs_ref[...])
    c_ref[...] = c
    kpe_ref[...] = kpe
    kcat_ref[:, :KV_RANK] = c.astype(BF16)
    kcat_ref[:, KV_RANK:] = kpe.astype(BF16)


def mm_kv(x, w_kv, g_kv, cs):
    m, k = x.shape
    n = w_kv.shape[0]
    tm = _tile(m, 1024)
    return pl.pallas_call(
        _mm_kv_kernel,
        grid=(m // tm,),
        in_specs=[pl.BlockSpec((tm, k), lambda i: (i, 0)), pl.BlockSpec((n, k), lambda i: (0, 0)),
                  pl.BlockSpec((1, KV_RANK), lambda i: (0, 0)), pl.BlockSpec((tm, LANES), lambda i: (i, 0))],
        out_specs=[pl.BlockSpec((tm, KV_RANK), lambda i: (i, 0)), pl.BlockSpec((tm, QK_ROPE), lambda i: (i, 0)),
                   pl.BlockSpec((tm, K_CAT), lambda i: (i, 0))],
        out_shape=[jax.ShapeDtypeStruct((m, KV_RANK), F32), jax.ShapeDtypeStruct((m, QK_ROPE), F32),
                   jax.ShapeDtypeStruct((m, K_CAT), BF16)],
        compiler_params=_params(1),
        name="mm_kv",
    )(x, w_kv, g_kv.reshape(1, KV_RANK), cs)


def _mm_q_kernel(x_ref, wq_ref, wn_ref, cs_ref, o_ref):
    x = x_ref[...]
    cs = cs_ref[...]
    groups = x.shape[0] // QBLOCK
    q_all = jnp.dot(x, wq_ref[...], preferred_element_type=F32)
    for h in range(HEADS):
        qh = q_all[:, h * 256:(h + 1) * 256]
        q_abs = jnp.dot(qh[:, :QK_NOPE].astype(BF16), wn_ref[h], preferred_element_type=F32) * ATTN_SCALE
        q_pe = _rope_from_pair(qh[:, QK_NOPE:], cs) * ATTN_SCALE
        o_ref[:, h, :, :KV_RANK] = q_abs.astype(BF16).reshape(groups, QBLOCK, KV_RANK)
        o_ref[:, h, :, KV_RANK:] = q_pe.astype(BF16).reshape(groups, QBLOCK, QK_ROPE)


def mm_q(qn, w_q, w_nope, cs):
    m, k = qn.shape
    tm = _tile(m, 512)
    return pl.pallas_call(
        _mm_q_kernel,
        grid=(m // tm,),
        in_specs=[pl.BlockSpec((tm, k), lambda i: (i, 0)),
                  pl.BlockSpec(w_q.shape, lambda i: (0, 0)),
                  pl.BlockSpec(w_nope.shape, lambda i: (0, 0, 0)),
                  pl.BlockSpec((tm, LANES), lambda i: (i, 0))],
        out_specs=pl.BlockSpec((tm // QBLOCK, HEADS, QBLOCK, K_CAT), lambda i: (i, 0, 0, 0)),
        out_shape=jax.ShapeDtypeStruct((m // QBLOCK, HEADS, QBLOCK, K_CAT), BF16),
        compiler_params=_params(1),
        name="mm_q",
    )(qn, w_q, w_nope, cs)


def _mm_glu_kernel(x_ref, wa_ref, wg_ref, o_ref):
    x = x_ref[...]
    a = _bf16_dot(x, wa_ref[...])
    b = _bf16_dot(x, wg_ref[...])
    o_ref[...] = a * _sigmoid(b)


def mm_glu(x, w, width):
    m, k = x.shape
    tm = _tile(m, ROW_TILE)
    tn = _tile(width, 256)
    nb = width // tn
    return pl.pallas_call(
        _mm_glu_kernel,
        grid=(m // tm, nb),
        in_specs=[pl.BlockSpec((tm, k), lambda i, j: (i, 0)),
                  pl.BlockSpec((k, tn), lambda i, j: (0, j)),
                  pl.BlockSpec((k, tn), lambda i, j: (0, j + nb))],
        out_specs=pl.BlockSpec((tm, tn), lambda i, j: (i, j)),
        out_shape=jax.ShapeDtypeStruct((m, width), F32),
        compiler_params=_params(2),
        name="mm_glu",
    )(x, w, w)


def _mix_a_kernel(gu_ref, gv_ref, sz_ref, lg_ref, lb_ref, wm_ref, bias_ref, oa_ref, vo_ref):
    vn = _layer_norm(gv_ref[...], lg_ref[...], lb_ref[...])
    vo_ref[...] = vn
    vb = vn.astype(BF16)
    for g in range(A_GROUPS):
        sl = slice(g * A_GROUP_DIM, (g + 1) * A_GROUP_DIM)
        mix = jnp.dot(wm_ref[g], vb[:, sl], preferred_element_type=F32) + bias_ref[:, sl]
        oa_ref[:, sl] = (gu_ref[:, sl].astype(F32) * mix * sz_ref[:, sl].astype(F32)).astype(BF16)


def _softmax_update(s, v, m_ref, l_ref, acc_ref):
    n_rep = s.shape[1] // LANES
    m_prev = m_ref[...]
    m_new = jnp.maximum(m_prev, jnp.max(s, axis=1, keepdims=True))
    alpha = jnp.exp(m_prev - m_new)
    p = jnp.exp(s - jnp.concatenate([m_new] * n_rep, axis=1))
    l_ref[...] = alpha * l_ref[...] + jnp.sum(p, axis=1, keepdims=True)
    m_ref[...] = m_new
    acc_ref[...] = (acc_ref[...] * jnp.concatenate([alpha] * (KV_RANK // LANES), axis=1)
                    + jnp.dot(p.astype(BF16), v, preferred_element_type=F32))


def _softmax_init(m_ref, l_ref, acc_ref):
    m_ref[...] = jnp.full(m_ref.shape, NEG_INF, F32)
    l_ref[...] = jnp.zeros(l_ref.shape, F32)
    acc_ref[...] = jnp.zeros(acc_ref.shape, F32)


def _softmax_result(l_ref, acc_ref):
    inv = 1.0 / l_ref[...]
    return acc_ref[...] * jnp.concatenate([inv] * (KV_RANK // LANES), axis=1)


def _nt_dot(a, b):
    return lax.dot_general(a, b, (((1,), (1,)), ((), ())), preferred_element_type=F32)


def _prefill_kernel(q_ref, k_ref, sz_ref, wv_ref, o_ref, m_ref, l_ref, acc_ref, *, kc):
    qb = pl.program_id(1)
    rows = HEADS * QBLOCK
    q = q_ref[0].reshape(rows, K_CAT)
    _softmax_init(m_ref, l_ref, acc_ref)

    def chunk(k0, width, masked):
        k = k_ref[0, pl.ds(k0, width), :]
        s = _nt_dot(q, k)
        if masked:
            q_pos = qb * QBLOCK + (lax.broadcasted_iota(jnp.int32, (rows, width), 0) & (QBLOCK - 1))
            k_pos = k0 + lax.broadcasted_iota(jnp.int32, (rows, width), 1)
            s = jnp.where(k_pos <= q_pos, s, NEG_INF)
        _softmax_update(s, k[:, :KV_RANK], m_ref, l_ref, acc_ref)

    n_full = (qb * QBLOCK) // kc

    def body(i, carry):
        chunk(pl.multiple_of(i * kc, kc), kc, False)
        return carry

    lax.fori_loop(0, n_full, body, 0)
    k_diag = pl.multiple_of(n_full * kc, kc)
    blocks_in = qb - n_full * (kc // QBLOCK)
    for w in range(kc // QBLOCK):
        @pl.when(blocks_in == w)
        def _():
            chunk(k_diag, (w + 1) * QBLOCK, True)

    o = _softmax_result(l_ref, acc_ref)
    for h in range(HEADS):
        sl = slice(h * V_DIM, (h + 1) * V_DIM)
        r = jnp.dot(o[h * QBLOCK:(h + 1) * QBLOCK].astype(BF16), wv_ref[h], preferred_element_type=F32)
        o_ref[0, :, sl] = (r * sz_ref[0, :, sl].astype(F32)).astype(BF16)


def mla_prefill(q, kcat, sz, w_v, zb_col_block):
    b, l, _ = kcat.shape
    nq = l // QBLOCK
    kc = _tile(l, 512)
    rows = HEADS * QBLOCK
    return pl.pallas_call(
        functools.partial(_prefill_kernel, kc=kc),
        grid=(b, nq),
        in_specs=[pl.BlockSpec((1, HEADS, QBLOCK, K_CAT), lambda i, j: (i * nq + j, 0, 0, 0)),
                  pl.BlockSpec((1, l, K_CAT), lambda i, j: (i, 0, 0)),
                  pl.BlockSpec((1, QBLOCK, W_B), lambda i, j: (i, j, zb_col_block)),
                  pl.BlockSpec(w_v.shape, lambda i, j: (0, 0, 0))],
        out_specs=pl.BlockSpec((1, QBLOCK, W_B), lambda i, j: (i, j, 0)),
        out_shape=jax.ShapeDtypeStruct((b, l, W_B), BF16),
        scratch_shapes=[pltpu.VMEM((rows, LANES), F32), pltpu.VMEM((rows, LANES), F32),
                        pltpu.VMEM((rows, KV_RANK), F32)],
        compiler_params=_params(2),
        name="mla_prefill",
    )(q, kcat, sz, w_v)


def _decode_kernel(pt_ref, q_ref, knew_ref, cc_hbm, cr_hbm, o_ref,
                   cbuf, rbuf, cb_ref, rb_ref, m_ref, l_ref, acc_ref, sem_c, sem_r,
                   *, layer, n_pages, ch, lq):
    b = pl.program_id(0)
    n_chunks = n_pages // ch
    total = pl.num_programs(0) * n_chunks
    ahead = DECODE_SLOTS - 1

    def chunk_copies(g):
        slot = lax.rem(g, DECODE_SLOTS)
        copies = []
        for p in range(ch):
            page = pt_ref[g * ch + p]
            copies.append(pltpu.make_async_copy(cc_hbm.at[layer, page], cbuf.at[slot, pl.ds(p * PAGE, PAGE)],
                                                sem_c.at[slot]))
            copies.append(pltpu.make_async_copy(cr_hbm.at[layer, page], rbuf.at[slot, p], sem_r.at[slot]))
        return copies

    def start_chunk(g):
        for n, cp in enumerate(chunk_copies(g)):
            cp.start(priority=(n // 2) % 2)

    @pl.when(b == 0)
    def _():
        for g in range(ahead):
            start_chunk(g)

    _softmax_init(m_ref, l_ref, acc_ref)
    q = q_ref[0]

    def chunk_body(j, carry):
        g = b * n_chunks + j

        @pl.when(g + ahead < total)
        def _():
            start_chunk(g + ahead)

        for cp in chunk_copies(g):
            cp.wait()
        slot = lax.rem(g, DECODE_SLOTS)
        cb_ref[...] = cbuf[slot].astype(BF16)
        for p in range(ch):
            rb_ref[:, p * PAGE:(p + 1) * PAGE] = rbuf[slot, p].astype(BF16)
        cb = cb_ref[...]
        s = _nt_dot(q[:, :KV_RANK], cb) + jnp.dot(q[:, KV_RANK:], rb_ref[...], preferred_element_type=F32)
        _softmax_update(s, cb, m_ref, l_ref, acc_ref)
        return carry

    lax.fori_loop(0, n_chunks, chunk_body, 0)

    kn = knew_ref[0]
    rows = q.shape[0]
    s_new = _nt_dot(q, kn)
    q_pos = lax.broadcasted_iota(jnp.int32, (rows, PAGE), 0) & (lq - 1)
    k_pos = lax.broadcasted_iota(jnp.int32, (rows, PAGE), 1)
    s_new = jnp.where(k_pos <= q_pos, s_new, NEG_INF)
    _softmax_update(s_new, kn[:, :KV_RANK], m_ref, l_ref, acc_ref)
    o_ref[0] = _softmax_result(l_ref, acc_ref).astype(BF16)


def mla_decode(q, knew, cache_c, cache_rt, layer, page_table):
    bd, rows, _ = q.shape
    lq = rows // HEADS
    assert lq & (lq - 1) == 0
    n_pages = page_table.shape[1]
    ch = DECODE_PAGES_PER_CHUNK if n_pages % DECODE_PAGES_PER_CHUNK == 0 else n_pages
    assert bd * (n_pages // ch) >= DECODE_SLOTS
    seq_block = lambda b, pt: (b, 0, 0)
    grid_spec = pltpu.PrefetchScalarGridSpec(
        num_scalar_prefetch=1,
        grid=(bd,),
        in_specs=[pl.BlockSpec((1, rows, K_CAT), seq_block),
                  pl.BlockSpec((1, PAGE, K_CAT), seq_block),
                  pl.BlockSpec(memory_space=pl.ANY),
                  pl.BlockSpec(memory_space=pl.ANY)],
        out_specs=pl.BlockSpec((1, rows, KV_RANK), seq_block),
        scratch_shapes=[pltpu.VMEM((DECODE_SLOTS, ch * PAGE, KV_RANK), F32),
                        pltpu.VMEM((DECODE_SLOTS, ch, QK_ROPE, PAGE), F32),
                        pltpu.VMEM((ch * PAGE, KV_RANK), BF16), pltpu.VMEM((QK_ROPE, ch * PAGE), BF16),
                        pltpu.VMEM((rows, LANES), F32), pltpu.VMEM((rows, LANES), F32),
                        pltpu.VMEM((rows, KV_RANK), F32),
                        pltpu.SemaphoreType.DMA((DECODE_SLOTS,)), pltpu.SemaphoreType.DMA((DECODE_SLOTS,))],
    )
    return pl.pallas_call(
        functools.partial(_decode_kernel, layer=layer, n_pages=n_pages, ch=ch, lq=lq),
        grid_spec=grid_spec,
        out_shape=jax.ShapeDtypeStruct((bd, rows, KV_RANK), BF16),
        compiler_params=_params_seq(("arbitrary",)),
        name="mla_decode",
    )(page_table.reshape(-1), q, knew, cache_c, cache_rt)


def _v_up_kernel(o_ref, wv_ref, sz_ref, out_ref):
    r = jnp.dot(o_ref[0], wv_ref[0], preferred_element_type=F32)
    out_ref[...] = (r * sz_ref[...].astype(F32)).astype(BF16)


def v_up(o_heads, w_v, sz, zb_col0):
    _, m, _ = o_heads.shape
    jb = zb_col0 // V_DIM
    return pl.pallas_call(
        _v_up_kernel,
        grid=(HEADS,),
        in_specs=[pl.BlockSpec((1, m, KV_RANK), lambda h: (h, 0, 0)),
                  pl.BlockSpec((1, KV_RANK, V_DIM), lambda h: (h, 0, 0)),
                  pl.BlockSpec((m, V_DIM), lambda h: (0, h + jb))],
        out_specs=pl.BlockSpec((m, V_DIM), lambda h: (0, h)),
        out_shape=jax.ShapeDtypeStruct((m, W_B), BF16),
        compiler_params=_params(1),
        name="v_up",
    )(o_heads, w_v, sz)


def _conv_kernel(prev_ref, g_ref, sz_ref, w_ref, b_ref, lg_ref, lb_ref, o_ref, xp_ref, xs_ref, d_ref,
                 *, tl, lane_chunk):
    c = g_ref.shape[-1]
    xp_ref[:CONV_HALO, :] = jnp.where(pl.program_id(1) == 0, 0.0, prev_ref[0])
    xp_ref[CONV_HALO:CONV_HALO + tl, :] = g_ref[0]
    base = CONV_HALO - (CONV_W - 1)

    def lane_body(ci, carry):
        c0 = pl.multiple_of(ci * lane_chunk, lane_chunk)
        lanes = pl.ds(c0, lane_chunk)
        for phase in range(1, SUBLANES):
            xs_ref[phase] = xp_ref[phase:phase + tl + CONV_HALO - SUBLANES, lanes]
        acc = jnp.broadcast_to(b_ref[:, lanes], (tl, lane_chunk))
        for k in range(CONV_W):
            phase = (base + k) % SUBLANES
            off = (base + k) - phase
            if phase == 0:
                x = xp_ref[off:off + tl, lanes]
            else:
                x = xs_ref[phase, off:off + tl, :]
            acc = acc + w_ref[k:k + 1, lanes] * x
        d_ref[:, lanes] = acc
        return carry

    lax.fori_loop(0, c // lane_chunk, lane_body, 0)

    rows = min(tl, LN_ROWS)

    def norm_body(ri, carry):
        r = pl.ds(pl.multiple_of(ri * rows, rows), rows)
        y = _layer_norm(d_ref[r, :], lg_ref[...], lb_ref[...])
        o_ref[0, r, :] = (_silu(y) * sz_ref[0, r, :].astype(F32)).astype(BF16)
        return carry

    lax.fori_loop(0, tl // rows, norm_body, 0)


def conv_module(g, sz, w_dw, b_dw, ln_g, ln_b):
    b, l, c = g.shape
    tl = _tile(l, CONV_TILE)
    per = tl // CONV_HALO
    vec = lambda i, j: (0, 0)
    lane_chunk = LANES
    return pl.pallas_call(
        functools.partial(_conv_kernel, tl=tl, lane_chunk=lane_chunk),
        grid=(b, l // tl),
        in_specs=[pl.BlockSpec((1, CONV_HALO, c), lambda i, j: (i, jnp.maximum(j * per - 1, 0), 0)),
                  pl.BlockSpec((1, tl, c), lambda i, j: (i, j, 0)),
                  pl.BlockSpec((1, tl, c), lambda i, j: (i, j, 0)),
                  pl.BlockSpec((CONV_W, c), vec), pl.BlockSpec((1, c), vec),
                  pl.BlockSpec((1, c), vec), pl.BlockSpec((1, c), vec)],
        out_specs=pl.BlockSpec((1, tl, c), lambda i, j: (i, j, 0)),
        out_shape=jax.ShapeDtypeStruct((b, l, c), BF16),
        scratch_shapes=[pltpu.VMEM((CONV_HALO + tl, c), F32),
                        pltpu.VMEM((SUBLANES, tl + CONV_HALO - SUBLANES, lane_chunk), F32),
                        pltpu.VMEM((tl, c), F32)],
        compiler_params=_params(2),
        name="conv_module",
    )(g, g, sz, w_dw, b_dw.reshape(1, c), ln_g.reshape(1, c), ln_b.reshape(1, c))


def _conv_step_kernel(st_ref, g_ref, sz_ref, w_ref, b_ref, lg_ref, lb_ref, o_ref, ns_ref, d_ref, *, lane_chunk):
    n_prev = st_ref.shape[0]
    lq, bb, c = g_ref.shape

    def row(j, lanes):
        return st_ref[j, :, lanes] if j < n_prev else g_ref[j - n_prev, :, lanes]

    def lane_body(ci, carry):
        lanes = pl.ds(pl.multiple_of(ci * lane_chunk, lane_chunk), lane_chunk)
        acc = [jnp.broadcast_to(b_ref[:, lanes], (bb, lane_chunk)) for _ in range(lq)]
        for j in range(n_prev - (CONV_W - 1), n_prev + lq):
            x = row(j, lanes)
            for t in range(lq):
                k = j - (n_prev - (CONV_W - 1)) - t
                if 0 <= k < CONV_W:
                    acc[t] = acc[t] + w_ref[k:k + 1, lanes] * x
        for t in range(lq):
            d_ref[t, :, lanes] = acc[t]
        return carry

    lax.fori_loop(0, c // lane_chunk, lane_body, 0)
    for t in range(lq):
        y = _layer_norm(d_ref[t], lg_ref[...], lb_ref[...])
        o_ref[t] = (_silu(y) * sz_ref[t].astype(F32)).astype(BF16)
    ns_ref[:n_prev - lq] = st_ref[lq:]
    ns_ref[n_prev - lq:] = g_ref[...]


def conv_step(state_t, g_t, sz_t, w_dw, b_dw, ln_g, ln_b):
    n_prev, b, c = state_t.shape
    lq = g_t.shape[0]
    assert n_prev == CONV_W - 1 and lq < n_prev
    bb = _tile(b, 16)
    vec = lambda i: (0, 0)
    blk = lambda i: (0, i, 0)
    return pl.pallas_call(
        functools.partial(_conv_step_kernel, lane_chunk=min(c, 4 * LANES)),
        grid=(b // bb,),
        in_specs=[pl.BlockSpec((n_prev, bb, c), blk), pl.BlockSpec((lq, bb, c), blk), pl.BlockSpec((lq, bb, c), blk),
                  pl.BlockSpec((CONV_W, c), vec), pl.BlockSpec((1, c), vec),
                  pl.BlockSpec((1, c), vec), pl.BlockSpec((1, c), vec)],
        out_specs=[pl.BlockSpec((lq, bb, c), blk), pl.BlockSpec((n_prev, bb, c), blk)],
        out_shape=[jax.ShapeDtypeStruct((lq, b, c), BF16), jax.ShapeDtypeStruct((n_prev, b, c), F32)],
        scratch_shapes=[pltpu.VMEM((lq, bb, c), F32)],
        compiler_params=_params(1),
        name="conv_step",
    )(state_t, g_t, sz_t, w_dw, b_dw.reshape(1, c), ln_g.reshape(1, c), ln_b.reshape(1, c))


def _rope_cs(length, offset, reps):
    inv = jnp.power(ROPE_THETA, -jnp.arange(0, QK_ROPE, 2, dtype=F32) / QK_ROPE)
    pos = jnp.arange(length, dtype=F32) + offset
    ang = pos[:, None] * inv[None, :]
    cos, sin = jnp.cos(ang), jnp.sin(ang)
    return jnp.tile(jnp.concatenate([cos, cos, -sin, sin], axis=-1), (reps, 1))


def _swap_halves(w):
    half = w.shape[-1] // 2
    return jnp.concatenate([w[..., half:], w[..., :half]], axis=-1)


def _even_weights(w_in, w_q_b, w_kv_b):
    wt = w_in.T
    o_qa = 3 * W_A
    o_kv = o_qa + Q_RANK
    o_kr = o_kv + KV_RANK
    o_zb = o_kr + QK_ROPE
    w_mid = lax.optimization_barrier(wt[o_qa:o_zb]).astype(BF16)
    w_qa = w_mid[:Q_RANK]
    w_kr = w_mid[Q_RANK + KV_RANK:]
    w_kr_sw = jnp.concatenate([w_kr[QK_ROPE // 2:], w_kr[:QK_ROPE // 2]], axis=0)
    w_kv = jnp.concatenate([w_mid[Q_RANK:], w_kr_sw], axis=0)
    wq = w_q_b.reshape(Q_RANK, HEADS, QK_NOPE + QK_ROPE)
    wq_rope = wq[..., QK_NOPE:]
    w_q = jnp.concatenate([wq[..., :QK_NOPE], wq_rope, _swap_halves(wq_rope)], axis=-1)
    w_q = w_q.reshape(Q_RANK, HEADS * 256).astype(BF16)
    wkvb = w_kv_b.reshape(KV_RANK, HEADS, QK_NOPE + V_DIM)
    w_nope = jnp.transpose(wkvb[..., :QK_NOPE], (1, 2, 0)).astype(BF16)
    w_vup = jnp.transpose(wkvb[..., QK_NOPE:], (1, 0, 2)).astype(BF16)
    return wt, w_qa, w_kv, w_q, w_nope, w_vup


def _spatial_weights(w_s, b_s, seq_len):
    cl = min(CHUNK, seq_len)
    reps = CHUNK // cl
    wm = (w_s * jnp.tril(jnp.ones((CHUNK, CHUNK), w_s.dtype)))[:, :cl, :cl]
    eye = jnp.eye(reps, dtype=w_s.dtype)
    wm = jnp.einsum("ab,gts->gatbs", eye, wm).reshape(A_GROUPS, CHUNK, CHUNK).astype(BF16)
    bias = jnp.tile(b_s[:, :cl].T, (reps, 1))
    bias = jnp.repeat(bias, A_GROUP_DIM, axis=1)
    return wm, bias


def _even_layer(x, h, batch, seq_len, offset, ew, w_out, ln_v_g, ln_v_b, w_s, b_s, g_q_a, g_kv_a, g_post, g_next,
                past):
    w_in_t, w_qa, w_kv, w_q, w_nope, w_vup = ew
    m = x.shape[0]
    z_b_row0 = 3 * W_A + Q_RANK + KV_RANK + QK_ROPE
    gates = mm_gates(h, w_in_t, (0, 2 * W_A, z_b_row0))
    gv = mm_act(h, w_in_t, col0=W_A, ncols=W_A, act=jax.nn.gelu, out_dtype=F32, transposed_w=True)
    qn = mm_rms(h, w_qa, g_q_a)
    cs = _rope_cs(seq_len, offset, batch)
    c, k_pe, kcat = mm_kv(h, w_kv, g_kv_a, cs)

    wm, bias = _spatial_weights(w_s, b_s, seq_len)
    blocks_per_seq = max(seq_len // CHUNK, 1)
    out_a, v_open = _mix_a_call(gates, gv, ln_v_g, ln_v_b, wm, bias, blocks_per_seq)

    q = mm_q(qn, w_q, w_nope, cs)
    if past is None:
        out_b = mla_prefill(q, kcat.reshape(batch, seq_len, K_CAT), gates.reshape(batch, seq_len, 3 * W_A),
                            w_vup, zb_col_block=2).reshape(m, W_B)
    else:
        cache_c, cache_rt, layer, page_table = past
        per = QBLOCK // seq_len
        qd = q.reshape(m // QBLOCK, HEADS, per, seq_len, K_CAT)
        qd = jnp.transpose(qd, (0, 2, 1, 3, 4)).reshape(batch, HEADS * seq_len, K_CAT)
        knew = jnp.pad(kcat.reshape(batch, seq_len, K_CAT), ((0, 0), (0, PAGE - seq_len), (0, 0)))
        o_lat = mla_decode(qd, knew, cache_c, cache_rt, layer, page_table)
        o_heads = jnp.transpose(o_lat.reshape(batch, HEADS, seq_len, KV_RANK), (1, 0, 2, 3))
        out_b = v_up(o_heads.reshape(HEADS, m, KV_RANK), w_vup, gates, zb_col0=2 * W_A)
    y = mm_plain([out_a, out_b], w_out, BF16)
    x_new, h_next = residual_norm(x, y, g_post, g_next)
    return x_new, h_next, c, k_pe, v_open


def _mix_a_call(gates, gv, ln_v_g, ln_v_b, wm, bias, blocks_per_seq):
    m = gv.shape[0]
    nblk = m // CHUNK
    row0 = lambda i: (i, 0)
    row1 = lambda i: (i, 1)
    fixed2 = lambda i: (0, 0)
    return pl.pallas_call(
        _mix_a_kernel,
        grid=(nblk,),
        in_specs=[pl.BlockSpec((CHUNK, W_A), row0), pl.BlockSpec((CHUNK, W_A), row0),
                  pl.BlockSpec((CHUNK, W_A), row1),
                  pl.BlockSpec((1, W_A), fixed2), pl.BlockSpec((1, W_A), fixed2),
                  pl.BlockSpec((A_GROUPS, CHUNK, CHUNK), lambda i: (0, 0, 0)),
                  pl.BlockSpec((CHUNK, W_A), fixed2)],
        out_specs=[pl.BlockSpec((CHUNK, W_A), row0),
                   pl.BlockSpec((CHUNK, W_A), lambda i: (i // blocks_per_seq, 0))],
        out_shape=[jax.ShapeDtypeStruct((m, W_A), BF16),
                   jax.ShapeDtypeStruct((m // blocks_per_seq, W_A), F32)],
        compiler_params=_params_seq(("arbitrary",)),
        name="mix_a",
    )(gates, gv, gates, ln_v_g.reshape(1, W_A), ln_v_b.reshape(1, W_A), wm, bias)


def _odd_layer(x, h, batch, seq_len, conv_prev, w_in, w_dw, b_dw, ln_c_g, ln_c_b, w_out, g_post, g_next):
    m = x.shape[0]
    c = w_dw.shape[1]
    g = mm_glu(h, w_in, c).reshape(batch, seq_len, c)
    sz = mm_act(h, w_in, col0=2 * c, ncols=c, act=_silu, out_dtype=BF16).reshape(batch, seq_len, c)
    if conv_prev is None:
        y2 = conv_module(g, sz, w_dw, b_dw, ln_c_g, ln_c_b)
        new_state = g[:, seq_len - (CONV_W - 1):]
    else:
        tm = lambda a: jnp.swapaxes(a, 0, 1)
        y2, new_state = conv_step(tm(conv_prev), tm(g), tm(sz), w_dw, b_dw, ln_c_g, ln_c_b)
        y2, new_state = tm(y2), tm(new_state)
    y = mm_plain([y2.reshape(m, c)], w_out, BF16)
    x_new, h_next = residual_norm(x, y, g_post, g_next)
    return x_new, h_next, new_state


def kernel(x_prompt, x_sample, cache_kv_latent, cache_k_rope, state_conv, page_table, w_in_even, ln_v_g, ln_v_b,
           w_s, b_s, g_q_a, w_q_b, g_kv_a, w_kv_b, w_out_even, w_in_odd, w_dw, b_dw, ln_c_g, ln_c_b, w_out_odd,
           g_pre, g_post):
    bp, lp, d = x_prompt.shape
    bs, ls, _ = x_sample.shape
    past_len = page_table.shape[1] * PAGE
    depth = g_pre.shape[0]
    xp = x_prompt.reshape(bp * lp, d)
    xs = x_sample.reshape(bs * ls, d)
    lat_p, kpe_p, v_p, conv_p = [], [], [], []
    lat_s, kpe_s, v_s, conv_s = [], [], [], []
    hp = rmsnorm_cast(xp, g_pre[0])
    hs = rmsnorm_cast(xs, g_pre[0])
    for i in range(depth):
        j = i // 2
        g_next = g_pre[i + 1] if i + 1 < depth else None
        if i % 2 == 0:
            ew = _even_weights(w_in_even[j], w_q_b[j], w_kv_b[j])
            args = (ew, w_out_even[j], ln_v_g[j], ln_v_b[j], w_s[j], b_s[j], g_q_a[j], g_kv_a[j], g_post[i], g_next)
            xp, hp, c, k, v = _even_layer(xp, hp, bp, lp, 0, *args, None)
            lat_p.append(c.reshape(bp, lp, KV_RANK))
            kpe_p.append(k.reshape(bp, lp, QK_ROPE))
            v_p.append(v.reshape(bp, -1, W_A))
            cache_rt = jnp.swapaxes(cache_k_rope, 2, 3)
            xs, hs, c, k, v = _even_layer(xs, hs, bs, ls, past_len, *args,
                                          (cache_kv_latent, cache_rt, j, page_table))
            lat_s.append(c.reshape(bs, ls, KV_RANK))
            kpe_s.append(k.reshape(bs, ls, QK_ROPE))
            v_s.append(v.reshape(bs, ls, W_A))
        else:
            args = (w_in_odd[j], w_dw[j], b_dw[j], ln_c_g[j], ln_c_b[j], w_out_odd[j], g_post[i], g_next)
            xp, hp, new_state = _odd_layer(xp, hp, bp, lp, None, *args)
            conv_p.append(new_state)
            xs, hs, new_state = _odd_layer(xs, hs, bs, ls, state_conv[j], *args)
            conv_s.append(new_state)
    return (xp.reshape(bp, lp, d), xs.reshape(bs, ls, d), jnp.stack(lat_p), jnp.stack(kpe_p), jnp.stack(v_p),
            jnp.stack(conv_p), jnp.stack(lat_s), jnp.stack(kpe_s), jnp.stack(v_s), jnp.stack(conv_s))
```

```python
import functools
import math

import jax
import jax.numpy as jnp
from jax import lax
from jax.experimental import pallas as pl
from jax.experimental.pallas import tpu as pltpu

F32 = jnp.float32
BF16 = jnp.bfloat16

EPS = 1e-6
LANES = 128
SUBLANES = 8
CHUNK = 128
A_GROUPS = 16
A_GROUP_DIM = 128
W_A = A_GROUPS * A_GROUP_DIM
HEADS = 16
QK_NOPE = 128
QK_ROPE = 64
V_DIM = 128
Q_RANK = 1024
KV_RANK = 512
W_B = HEADS * V_DIM
K_CAT = KV_RANK + QK_ROPE
ROPE_THETA = 10000.0
QBLOCK = 128
PAGE = 128
DECODE_PAGES_PER_CHUNK = 16
DECODE_SLOTS = 4
CONV_W = 31
CONV_HALO = 32
CONV_TILE = 256
LN_ROWS = 64
ATTN_SCALE = 1.0 / math.sqrt(QK_NOPE + QK_ROPE)
NEG_INF = float("-inf")
VMEM_LIMIT_BYTES = 56 * 1024 * 1024


def _params(n_axes):
    return pltpu.CompilerParams(dimension_semantics=("parallel",) * n_axes,
                                vmem_limit_bytes=VMEM_LIMIT_BYTES)


def _params_seq(sem):
    return pltpu.CompilerParams(dimension_semantics=sem, vmem_limit_bytes=VMEM_LIMIT_BYTES)


def _sigmoid(x):
    return 1.0 / (1.0 + jnp.exp(-x))


def _silu(x):
    return x * _sigmoid(x)


def _rms(x, g):
    return x * lax.rsqrt(jnp.mean(x * x, axis=-1, keepdims=True) + EPS) * g


def _layer_norm(x, g, b):
    mu = jnp.mean(x, axis=-1, keepdims=True)
    xc = x - mu
    return xc * lax.rsqrt(jnp.mean(xc * xc, axis=-1, keepdims=True) + EPS) * g + b


def _tile(n, pref):
    return pref if n % pref == 0 else n


def _rmsnorm_kernel(x_ref, g_ref, o_ref):
    o_ref[...] = _rms(x_ref[...], g_ref[...]).astype(o_ref.dtype)


def rmsnorm_cast(x, g):
    m, d = x.shape
    tm = _tile(m, 256)
    return pl.pallas_call(
        _rmsnorm_kernel,
        grid=(m // tm,),
        in_specs=[pl.BlockSpec((tm, d), lambda i: (i, 0)), pl.BlockSpec((1, d), lambda i: (0, 0))],
        out_specs=pl.BlockSpec((tm, d), lambda i: (i, 0)),
        out_shape=jax.ShapeDtypeStruct((m, d), BF16),
        compiler_params=_params(1),
        name="rmsnorm_cast",
    )(x, g.reshape(1, d))


def _residual_norm_kernel(x_ref, y_ref, g_ref, o_ref):
    o_ref[...] = x_ref[...] + _rms(y_ref[...].astype(F32), g_ref[...])


def _residual_norm_next_kernel(x_ref, y_ref, g_ref, gn_ref, o_ref, h_ref):
    x_new = x_ref[...] + _rms(y_ref[...].astype(F32), g_ref[...])
    o_ref[...] = x_new
    h_ref[...] = _rms(x_new, gn_ref[...]).astype(h_ref.dtype)


def residual_norm(x, y, g, g_next=None):
    m, d = x.shape
    tm = _tile(m, 256)
    row = pl.BlockSpec((tm, d), lambda i: (i, 0))
    vec = pl.BlockSpec((1, d), lambda i: (0, 0))
    if g_next is None:
        return pl.pallas_call(
            _residual_norm_kernel,
            grid=(m // tm,),
            in_specs=[row, row, vec],
            out_specs=row,
            out_shape=jax.ShapeDtypeStruct((m, d), F32),
            compiler_params=_params(1),
            name="residual_norm",
        )(x, y, g.reshape(1, d)), None
    return pl.pallas_call(
        _residual_norm_next_kernel,
        grid=(m // tm,),
        in_specs=[row, row, vec, vec],
        out_specs=[row, row],
        out_shape=[jax.ShapeDtypeStruct((m, d), F32), jax.ShapeDtypeStruct((m, d), BF16)],
        compiler_params=_params(1),
        name="residual_norm_next",
    )(x, y, g.reshape(1, d), g_next.reshape(1, d))


ROW_TILE = 1024


def _col_tile(m, n):
    return _tile(n, 512 if m >= ROW_TILE else 1024)


def _bf16_dot(x, w, transposed_w=False):
    w = w.astype(BF16)
    if transposed_w:
        return _nt_dot(x, w)
    return jnp.dot(x, w, preferred_element_type=F32)


def _act_tile(x_ref, w_ref, o_ref, act, transposed_w):
    o_ref[...] = act(_bf16_dot(x_ref[...], w_ref[...], transposed_w)).astype(o_ref.dtype)


def _mm_act_kernel(x_ref, w_ref, o_ref, *, act, transposed_w):
    _act_tile(x_ref, w_ref, o_ref, act, transposed_w)


def mm_act(x, w, *, col0, ncols, act, out_dtype, transposed_w=False):
    m, k = x.shape
    tm = _tile(m, ROW_TILE)
    tn = _col_tile(m, ncols)
    assert col0 % tn == 0
    j0 = col0 // tn
    if transposed_w:
        w_spec = pl.BlockSpec((tn, k), lambda i, j: (j + j0, 0))
    else:
        w_spec = pl.BlockSpec((k, tn), lambda i, j: (0, j + j0))
    return pl.pallas_call(
        functools.partial(_mm_act_kernel, act=act, transposed_w=transposed_w),
        grid=(m // tm, ncols // tn),
        in_specs=[pl.BlockSpec((tm, k), lambda i, j: (i, 0)), w_spec],
        out_specs=pl.BlockSpec((tm, tn), lambda i, j: (i, j)),
        out_shape=jax.ShapeDtypeStruct((m, ncols), out_dtype),
        compiler_params=_params(2),
        name="mm_act",
    )(x, w)


def _mm_gates_kernel(x_ref, w_ref, o_ref, *, n_gelu_tiles):
    @pl.when(pl.program_id(1) < n_gelu_tiles)
    def _():
        _act_tile(x_ref, w_ref, o_ref, jax.nn.gelu, True)

    @pl.when(pl.program_id(1) >= n_gelu_tiles)
    def _():
        _act_tile(x_ref, w_ref, o_ref, _silu, True)


def mm_gates(x, w_in_t, row_starts):
    m, k = x.shape
    tm = _tile(m, ROW_TILE)
    tn = _tile(W_A, 512)
    per_gate = W_A // tn
    unit = QK_ROPE
    assert all(r % unit == 0 for r in row_starts) and tn % unit == 0

    def w_map(i, j):
        start = (row_starts[-1] - (len(row_starts) - 1) * W_A) // unit
        for g in range(len(row_starts) - 2, -1, -1):
            start = jnp.where(j < (g + 1) * per_gate, (row_starts[g] - g * W_A) // unit, start)
        return ((start + j * (tn // unit)) * unit, 0)

    return pl.pallas_call(
        functools.partial(_mm_gates_kernel, n_gelu_tiles=2 * per_gate),
        grid=(m // tm, len(row_starts) * per_gate),
        in_specs=[pl.BlockSpec((tm, k), lambda i, j: (i, 0)),
                  pl.BlockSpec((pl.Element(tn), pl.Element(k)), w_map)],
        out_specs=pl.BlockSpec((tm, tn), lambda i, j: (i, j)),
        out_shape=jax.ShapeDtypeStruct((m, len(row_starts) * W_A), BF16),
        compiler_params=_params(2),
        name="mm_gates",
    )(x, w_in_t)


def _mm_plain_kernel(*refs, n_in):
    x_refs = refs[:n_in]
    w_refs = refs[n_in:2 * n_in]
    o_ref = refs[2 * n_in]
    acc = _bf16_dot(x_refs[0][...], w_refs[0][...])
    for x_ref, w_ref in zip(x_refs[1:], w_refs[1:]):
        acc += _bf16_dot(x_ref[...], w_ref[...])
    o_ref[...] = acc.astype(o_ref.dtype)


def mm_plain(xs, w, out_dtype):
    m = xs[0].shape[0]
    n = w.shape[1]
    tm = _tile(m, ROW_TILE)
    tn = _col_tile(m, n)
    in_specs = [pl.BlockSpec((tm, x.shape[1]), lambda i, j: (i, 0)) for x in xs]
    row = 0
    for x in xs:
        kx = x.shape[1]
        assert row % kx == 0
        in_specs.append(pl.BlockSpec((kx, tn), functools.partial(lambda i, j, r: (r, j), r=row // kx)))
        row += kx
    assert row == w.shape[0]
    return pl.pallas_call(
        functools.partial(_mm_plain_kernel, n_in=len(xs)),
        grid=(m // tm, n // tn),
        in_specs=in_specs,
        out_specs=pl.BlockSpec((tm, tn), lambda i, j: (i, j)),
        out_shape=jax.ShapeDtypeStruct((m, n), out_dtype),
        compiler_params=_params(2),
        name="mm_plain",
    )(*xs, *([w] * len(xs)))


def _mm_rms_kernel(x_ref, w_ref, g_ref, o_ref):
    acc = _nt_dot(x_ref[...], w_ref[...])
    o_ref[...] = _rms(acc, g_ref[...]).astype(o_ref.dtype)


def mm_rms(x, w_t, g):
    m, k = x.shape
    n = w_t.shape[0]
    tm = _tile(m, 1024)
    return pl.pallas_call(
        _mm_rms_kernel,
        grid=(m // tm,),
        in_specs=[pl.BlockSpec((tm, k), lambda i: (i, 0)), pl.BlockSpec((n, k), lambda i: (0, 0)),
                  pl.BlockSpec((1, n), lambda i: (0, 0))],
        out_specs=pl.BlockSpec((tm, n), lambda i: (i, 0)),
        out_shape=jax.ShapeDtypeStruct((m, n), BF16),
        compiler_params=_params(1),
        name="mm_rms",
    )(x, w_t, g.reshape(1, n))


def _rope_from_pair(pair, cs):
    t = pair * cs
    return (t + pltpu.roll(t, QK_ROPE, axis=1))[:, :QK_ROPE]


def _mm_kv_kernel(x_ref, w_ref, g_ref, cs_ref, c_ref, kpe_ref, kcat_ref):
    acc = _nt_dot(x_ref[...], w_ref[...])
    c = _rms(acc[:, :KV_RANK], g_ref[...])
    kpe = _rope_from_pair(acc[:, KV_RANK:], cs_ref[...])
    c_ref[...] = c
    kpe_ref[...] = kpe
    kcat_ref[:, :KV_RANK] = c.astype(BF16)
    kcat_ref[:, KV_RANK:] = kpe.astype(BF16)


def mm_kv(x, w_kv, g_kv, cs):
    m, k = x.shape
    n = w_kv.shape[0]
    tm = _tile(m, 1024)
    return pl.pallas_call(
        _mm_kv_kernel,
        grid=(m // tm,),
        in_specs=[pl.BlockSpec((tm, k), lambda i: (i, 0)), pl.BlockSpec((n, k), lambda i: (0, 0)),
                  pl.BlockSpec((1, KV_RANK), lambda i: (0, 0)), pl.BlockSpec((tm, LANES), lambda i: (i, 0))],
        out_specs=[pl.BlockSpec((tm, KV_RANK), lambda i: (i, 0)), pl.BlockSpec((tm, QK_ROPE), lambda i: (i, 0)),
                   pl.BlockSpec((tm, K_CAT), lambda i: (i, 0))],
        out_shape=[jax.ShapeDtypeStruct((m, KV_RANK), F32), jax.ShapeDtypeStruct((m, QK_ROPE), F32),
                   jax.ShapeDtypeStruct((m, K_CAT), BF16)],
        compiler_params=_params(1),
        name="mm_kv",
    )(x, w_kv, g_kv.reshape(1, KV_RANK), cs)


def _mm_q_kernel(x_ref, wq_ref, wn_ref, cs_ref, o_ref):
    x = x_ref[...]
    cs = cs_ref[...]
    groups = x.shape[0] // QBLOCK
    q_all = jnp.dot(x, wq_ref[...], preferred_element_type=F32)
    for h in range(HEADS):
        qh = q_all[:, h * 256:(h + 1) * 256]
        q_abs = jnp.dot(qh[:, :QK_NOPE].astype(BF16), wn_ref[h], preferred_element_type=F32) * ATTN_SCALE
        q_pe = _rope_from_pair(qh[:, QK_NOPE:], cs) * ATTN_SCALE
        o_ref[:, h, :, :KV_RANK] = q_abs.astype(BF16).reshape(groups, QBLOCK, KV_RANK)
        o_ref[:, h, :, KV_RANK:] = q_pe.astype(BF16).reshape(groups, QBLOCK, QK_ROPE)


def mm_q(qn, w_q, w_nope, cs):
    m, k = qn.shape
    tm = _tile(m, 512)
    return pl.pallas_call(
        _mm_q_kernel,
        grid=(m // tm,),
        in_specs=[pl.BlockSpec((tm, k), lambda i: (i, 0)),
                  pl.BlockSpec(w_q.shape, lambda i: (0, 0)),
                  pl.BlockSpec(w_nope.shape, lambda i: (0, 0, 0)),
                  pl.BlockSpec((tm, LANES), lambda i: (i, 0))],
        out_specs=pl.BlockSpec((tm // QBLOCK, HEADS, QBLOCK, K_CAT), lambda i: (i, 0, 0, 0)),
        out_shape=jax.ShapeDtypeStruct((m // QBLOCK, HEADS, QBLOCK, K_CAT), BF16),
        compiler_params=_params(1),
        name="mm_q",
    )(qn, w_q, w_nope, cs)


def _mm_glu_kernel(x_ref, wa_ref, wg_ref, o_ref):
    x = x_ref[...]
    a = _bf16_dot(x, wa_ref[...])
    b = _bf16_dot(x, wg_ref[...])
    o_ref[...] = a * _sigmoid(b)


def mm_glu(x, w, width):
    m, k = x.shape
    tm = _tile(m, ROW_TILE)
    tn = _tile(width, 256)
    nb = width // tn
    return pl.pallas_call(
        _mm_glu_kernel,
        grid=(m // tm, nb),
        in_specs=[pl.BlockSpec((tm, k), lambda i, j: (i, 0)),
                  pl.BlockSpec((k, tn), lambda i, j: (0, j)),
                  pl.BlockSpec((k, tn), lambda i, j: (0, j + nb))],
        out_specs=pl.BlockSpec((tm, tn), lambda i, j: (i, j)),
        out_shape=jax.ShapeDtypeStruct((m, width), F32),
        compiler_params=_params(2),
        name="mm_glu",
    )(x, w, w)


def _mix_a_kernel(gu_ref, gv_ref, sz_ref, lg_ref, lb_ref, wm_ref, bias_ref, oa_ref, vo_ref):
    vn = _layer_norm(gv_ref[...].astype(F32), lg_ref[...], lb_ref[...])
    vo_ref[...] = vn
    vb = vn.astype(BF16)
    for g in range(A_GROUPS):
        sl = slice(g * A_GROUP_DIM, (g + 1) * A_GROUP_DIM)
        mix = jnp.dot(wm_ref[g], vb[:, sl], preferred_element_type=F32) + bias_ref[:, sl]
        oa_ref[:, sl] = (gu_ref[:, sl].astype(F32) * mix * sz_ref[:, sl].astype(F32)).astype(BF16)


def _softmax_update(s, v, m_ref, l_ref, acc_ref):
    n_rep = s.shape[1] // LANES
    m_prev = m_ref[...]
    m_new = jnp.maximum(m_prev, jnp.max(s, axis=1, keepdims=True))
    alpha = jnp.exp(m_prev - m_new)
    p = jnp.exp(s - jnp.concatenate([m_new] * n_rep, axis=1))
    l_ref[...] = alpha * l_ref[...] + jnp.sum(p, axis=1, keepdims=True)
    m_ref[...] = m_new
    acc_ref[...] = (acc_ref[...] * jnp.concatenate([alpha] * (KV_RANK // LANES), axis=1)
                    + jnp.dot(p.astype(BF16), v, preferred_element_type=F32))


def _softmax_init(m_ref, l_ref, acc_ref):
    m_ref[...] = jnp.full(m_ref.shape, NEG_INF, F32)
    l_ref[...] = jnp.zeros(l_ref.shape, F32)
    acc_ref[...] = jnp.zeros(acc_ref.shape, F32)


def _softmax_result(l_ref, acc_ref):
    inv = 1.0 / l_ref[...]
    return acc_ref[...] * jnp.concatenate([inv] * (KV_RANK // LANES), axis=1)


def _nt_dot(a, b):
    return lax.dot_general(a, b, (((1,), (1,)), ((), ())), preferred_element_type=F32)


def _prefill_kernel(q_ref, k_ref, sz_ref, wv_ref, o_ref, m_ref, l_ref, acc_ref, *, kc):
    qb = pl.program_id(1)
    rows = HEADS * QBLOCK
    q = q_ref[0].reshape(rows, K_CAT)
    _softmax_init(m_ref, l_ref, acc_ref)

    def chunk(k0, width, masked):
        k = k_ref[0, pl.ds(k0, width), :]
        s = _nt_dot(q, k)
        if masked:
            q_pos = qb * QBLOCK + (lax.broadcasted_iota(jnp.int32, (rows, width), 0) & (QBLOCK - 1))
            k_pos = k0 + lax.broadcasted_iota(jnp.int32, (rows, width), 1)
            s = jnp.where(k_pos <= q_pos, s, NEG_INF)
        _softmax_update(s, k[:, :KV_RANK], m_ref, l_ref, acc_ref)

    n_full = (qb * QBLOCK) // kc

    def body(i, carry):
        chunk(pl.multiple_of(i * kc, kc), kc, False)
        return carry

    lax.fori_loop(0, n_full, body, 0)
    k_diag = pl.multiple_of(n_full * kc, kc)
    blocks_in = qb - n_full * (kc // QBLOCK)
    for w in range(kc // QBLOCK):
        @pl.when(blocks_in == w)
        def _():
            chunk(k_diag, (w + 1) * QBLOCK, True)

    o = _softmax_result(l_ref, acc_ref)
    for h in range(HEADS):
        sl = slice(h * V_DIM, (h + 1) * V_DIM)
        r = jnp.dot(o[h * QBLOCK:(h + 1) * QBLOCK].astype(BF16), wv_ref[h], preferred_element_type=F32)
        o_ref[0, :, sl] = (r * sz_ref[0, :, sl].astype(F32)).astype(BF16)


def mla_prefill(q, kcat, sz, w_v, zb_col_block):
    b, l, _ = kcat.shape
    nq = l // QBLOCK
    kc = _tile(l, 512)
    rows = HEADS * QBLOCK
    return pl.pallas_call(
        functools.partial(_prefill_kernel, kc=kc),
        grid=(b, nq),
        in_specs=[pl.BlockSpec((1, HEADS, QBLOCK, K_CAT), lambda i, j: (i * nq + j, 0, 0, 0)),
                  pl.BlockSpec((1, l, K_CAT), lambda i, j: (i, 0, 0)),
                  pl.BlockSpec((1, QBLOCK, W_B), lambda i, j: (i, j, zb_col_block)),
                  pl.BlockSpec(w_v.shape, lambda i, j: (0, 0, 0))],
        out_specs=pl.BlockSpec((1, QBLOCK, W_B), lambda i, j: (i, j, 0)),
        out_shape=jax.ShapeDtypeStruct((b, l, W_B), BF16),
        scratch_shapes=[pltpu.VMEM((rows, LANES), F32), pltpu.VMEM((rows, LANES), F32),
                        pltpu.VMEM((rows, KV_RANK), F32)],
        compiler_params=_params(2),
        name="mla_prefill",
    )(q, kcat, sz, w_v)


def _decode_kernel(pt_ref, q_ref, knew_ref, cc_hbm, cr_hbm, o_ref,
                   cbuf, rbuf, cb_ref, rb_ref, m_ref, l_ref, acc_ref, sem_c, sem_r,
                   *, layer, n_pages, ch, lq):
    b = pl.program_id(0)
    n_chunks = n_pages // ch
    total = pl.num_programs(0) * n_chunks
    ahead = DECODE_SLOTS - 1

    def chunk_copies(g):
        slot = lax.rem(g, DECODE_SLOTS)
        copies = []
        for p in range(ch):
            page = pt_ref[g * ch + p]
            copies.append(pltpu.make_async_copy(cc_hbm.at[layer, page], cbuf.at[slot, pl.ds(p * PAGE, PAGE)],
                                                sem_c.at[slot]))
            copies.append(pltpu.make_async_copy(cr_hbm.at[layer, page], rbuf.at[slot, p], sem_r.at[slot]))
        return copies

    def start_chunk(g):
        for n, cp in enumerate(chunk_copies(g)):
            cp.start(priority=(n // 2) % 2)

    @pl.when(b == 0)
    def _():
        for g in range(ahead):
            start_chunk(g)

    _softmax_init(m_ref, l_ref, acc_ref)
    q = q_ref[0]

    def chunk_body(j, carry):
        g = b * n_chunks + j

        @pl.when(g + ahead < total)
        def _():
            start_chunk(g + ahead)

        for cp in chunk_copies(g):
            cp.wait()
        slot = lax.rem(g, DECODE_SLOTS)
        cb_ref[...] = cbuf[slot].astype(BF16)
        for p in range(ch):
            rb_ref[:, p * PAGE:(p + 1) * PAGE] = rbuf[slot, p].astype(BF16)
        cb = cb_ref[...]
        s = _nt_dot(q[:, :KV_RANK], cb) + jnp.dot(q[:, KV_RANK:], rb_ref[...], preferred_element_type=F32)
        _softmax_update(s, cb, m_ref, l_ref, acc_ref)
        return carry

    lax.fori_loop(0, n_chunks, chunk_body, 0)

    kn = knew_ref[0]
    rows = q.shape[0]
    s_new = _nt_dot(q, kn)
    q_pos = lax.broadcasted_iota(jnp.int32, (rows, PAGE), 0) & (lq - 1)
    k_pos = lax.broadcasted_iota(jnp.int32, (rows, PAGE), 1)
    s_new = jnp.where(k_pos <= q_pos, s_new, NEG_INF)
    _softmax_update(s_new, kn[:, :KV_RANK], m_ref, l_ref, acc_ref)
    o_ref[0] = _softmax_result(l_ref, acc_ref).astype(BF16)


def mla_decode(q, knew, cache_c, cache_rt, layer, page_table):
    bd, rows, _ = q.shape
    lq = rows // HEADS
    assert lq & (lq - 1) == 0
    n_pages = page_table.shape[1]
    ch = DECODE_PAGES_PER_CHUNK if n_pages % DECODE_PAGES_PER_CHUNK == 0 else n_pages
    assert bd * (n_pages // ch) >= DECODE_SLOTS
    seq_block = lambda b, pt: (b, 0, 0)
    grid_spec = pltpu.PrefetchScalarGridSpec(
        num_scalar_prefetch=1,
        grid=(bd,),
        in_specs=[pl.BlockSpec((1, rows, K_CAT), seq_block),
                  pl.BlockSpec((1, PAGE, K_CAT), seq_block),
                  pl.BlockSpec(memory_space=pl.ANY),
                  pl.BlockSpec(memory_space=pl.ANY)],
        out_specs=pl.BlockSpec((1, rows, KV_RANK), seq_block),
        scratch_shapes=[pltpu.VMEM((DECODE_SLOTS, ch * PAGE, KV_RANK), F32),
                        pltpu.VMEM((DECODE_SLOTS, ch, QK_ROPE, PAGE), F32),
                        pltpu.VMEM((ch * PAGE, KV_RANK), BF16), pltpu.VMEM((QK_ROPE, ch * PAGE), BF16),
                        pltpu.VMEM((rows, LANES), F32), pltpu.VMEM((rows, LANES), F32),
                        pltpu.VMEM((rows, KV_RANK), F32),
                        pltpu.SemaphoreType.DMA((DECODE_SLOTS,)), pltpu.SemaphoreType.DMA((DECODE_SLOTS,))],
    )
    return pl.pallas_call(
        functools.partial(_decode_kernel, layer=layer, n_pages=n_pages, ch=ch, lq=lq),
        grid_spec=grid_spec,
        out_shape=jax.ShapeDtypeStruct((bd, rows, KV_RANK), BF16),
        compiler_params=_params_seq(("arbitrary",)),
        name="mla_decode",
    )(page_table.reshape(-1), q, knew, cache_c, cache_rt)


def _v_up_kernel(o_ref, wv_ref, sz_ref, out_ref):
    r = jnp.dot(o_ref[0], wv_ref[0], preferred_element_type=F32)
    out_ref[...] = (r * sz_ref[...].astype(F32)).astype(BF16)


def v_up(o_heads, w_v, sz, zb_col0):
    _, m, _ = o_heads.shape
    jb = zb_col0 // V_DIM
    return pl.pallas_call(
        _v_up_kernel,
        grid=(HEADS,),
        in_specs=[pl.BlockSpec((1, m, KV_RANK), lambda h: (h, 0, 0)),
                  pl.BlockSpec((1, KV_RANK, V_DIM), lambda h: (h, 0, 0)),
                  pl.BlockSpec((m, V_DIM), lambda h: (0, h + jb))],
        out_specs=pl.BlockSpec((m, V_DIM), lambda h: (0, h)),
        out_shape=jax.ShapeDtypeStruct((m, W_B), BF16),
        compiler_params=_params(1),
        name="v_up",
    )(o_heads, w_v, sz)


def _conv_kernel(prev_ref, g_ref, sz_ref, w_ref, b_ref, lg_ref, lb_ref, o_ref, xp_ref, xs_ref, d_ref,
                 *, tl, lane_chunk):
    c = g_ref.shape[-1]
    xp_ref[:CONV_HALO, :] = jnp.where(pl.program_id(1) == 0, 0.0, prev_ref[0])
    xp_ref[CONV_HALO:CONV_HALO + tl, :] = g_ref[0]
    base = CONV_HALO - (CONV_W - 1)

    def lane_body(ci, carry):
        c0 = pl.multiple_of(ci * lane_chunk, lane_chunk)
        lanes = pl.ds(c0, lane_chunk)
        for phase in range(1, SUBLANES):
            xs_ref[phase] = xp_ref[phase:phase + tl + CONV_HALO - SUBLANES, lanes]
        acc = jnp.broadcast_to(b_ref[:, lanes], (tl, lane_chunk))
        for k in range(CONV_W):
            phase = (base + k) % SUBLANES
            off = (base + k) - phase
            if phase == 0:
                x = xp_ref[off:off + tl, lanes]
            else:
                x = xs_ref[phase, off:off + tl, :]
            acc = acc + w_ref[k:k + 1, lanes] * x
        d_ref[:, lanes] = acc
        return carry

    lax.fori_loop(0, c // lane_chunk, lane_body, 0)

    rows = min(tl, LN_ROWS)

    def norm_body(ri, carry):
        r = pl.ds(pl.multiple_of(ri * rows, rows), rows)
        y = _layer_norm(d_ref[r, :], lg_ref[...], lb_ref[...])
        o_ref[0, r, :] = (_silu(y) * sz_ref[0, r, :].astype(F32)).astype(BF16)
        return carry

    lax.fori_loop(0, tl // rows, norm_body, 0)


def conv_module(g, sz, w_dw, b_dw, ln_g, ln_b):
    b, l, c = g.shape
    tl = _tile(l, CONV_TILE)
    per = tl // CONV_HALO
    vec = lambda i, j: (0, 0)
    lane_chunk = LANES
    return pl.pallas_call(
        functools.partial(_conv_kernel, tl=tl, lane_chunk=lane_chunk),
        grid=(b, l // tl),
        in_specs=[pl.BlockSpec((1, CONV_HALO, c), lambda i, j: (i, jnp.maximum(j * per - 1, 0), 0)),
                  pl.BlockSpec((1, tl, c), lambda i, j: (i, j, 0)),
                  pl.BlockSpec((1, tl, c), lambda i, j: (i, j, 0)),
                  pl.BlockSpec((CONV_W, c), vec), pl.BlockSpec((1, c), vec),
                  pl.BlockSpec((1, c), vec), pl.BlockSpec((1, c), vec)],
        out_specs=pl.BlockSpec((1, tl, c), lambda i, j: (i, j, 0)),
        out_shape=jax.ShapeDtypeStruct((b, l, c), BF16),
        scratch_shapes=[pltpu.VMEM((CONV_HALO + tl, c), F32),
                        pltpu.VMEM((SUBLANES, tl + CONV_HALO - SUBLANES, lane_chunk), F32),
                        pltpu.VMEM((tl, c), F32)],
        compiler_params=_params(2),
        name="conv_module",
    )(g, g, sz, w_dw, b_dw.reshape(1, c), ln_g.reshape(1, c), ln_b.reshape(1, c))


def _conv_step_kernel(st_ref, g_ref, sz_ref, w_ref, b_ref, lg_ref, lb_ref, o_ref, ns_ref, d_ref, *, lane_chunk):
    n_prev = st_ref.shape[0]
    lq, bb, c = g_ref.shape

    def row(j, lanes):
        return st_ref[j, :, lanes] if j < n_prev else g_ref[j - n_prev, :, lanes]

    def lane_body(ci, carry):
        lanes = pl.ds(pl.multiple_of(ci * lane_chunk, lane_chunk), lane_chunk)
        acc = [jnp.broadcast_to(b_ref[:, lanes], (bb, lane_chunk)) for _ in range(lq)]
        for j in range(n_prev - (CONV_W - 1), n_prev + lq):
            x = row(j, lanes)
            for t in range(lq):
                k = j - (n_prev - (CONV_W - 1)) - t
                if 0 <= k < CONV_W:
                    acc[t] = acc[t] + w_ref[k:k + 1, lanes] * x
        for t in range(lq):
            d_ref[t, :, lanes] = acc[t]
        return carry

    lax.fori_loop(0, c // lane_chunk, lane_body, 0)
    for t in range(lq):
        y = _layer_norm(d_ref[t], lg_ref[...], lb_ref[...])
        o_ref[t] = (_silu(y) * sz_ref[t].astype(F32)).astype(BF16)
    ns_ref[:n_prev - lq] = st_ref[lq:]
    ns_ref[n_prev - lq:] = g_ref[...]


def conv_step(state_t, g_t, sz_t, w_dw, b_dw, ln_g, ln_b):
    n_prev, b, c = state_t.shape
    lq = g_t.shape[0]
    assert n_prev == CONV_W - 1 and lq < n_prev
    bb = _tile(b, 16)
    vec = lambda i: (0, 0)
    blk = lambda i: (0, i, 0)
    return pl.pallas_call(
        functools.partial(_conv_step_kernel, lane_chunk=min(c, 4 * LANES)),
        grid=(b // bb,),
        in_specs=[pl.BlockSpec((n_prev, bb, c), blk), pl.BlockSpec((lq, bb, c), blk), pl.BlockSpec((lq, bb, c), blk),
                  pl.BlockSpec((CONV_W, c), vec), pl.BlockSpec((1, c), vec),
                  pl.BlockSpec((1, c), vec), pl.BlockSpec((1, c), vec)],
        out_specs=[pl.BlockSpec((lq, bb, c), blk), pl.BlockSpec((n_prev, bb, c), blk)],
        out_shape=[jax.ShapeDtypeStruct((lq, b, c), BF16), jax.ShapeDtypeStruct((n_prev, b, c), F32)],
        scratch_shapes=[pltpu.VMEM((lq, bb, c), F32)],
        compiler_params=_params(1),
        name="conv_step",
    )(state_t, g_t, sz_t, w_dw, b_dw.reshape(1, c), ln_g.reshape(1, c), ln_b.reshape(1, c))


def _rope_cs(length, offset, reps):
    inv = jnp.power(ROPE_THETA, -jnp.arange(0, QK_ROPE, 2, dtype=F32) / QK_ROPE)
    pos = jnp.arange(length, dtype=F32) + offset
    ang = pos[:, None] * inv[None, :]
    cos, sin = jnp.cos(ang), jnp.sin(ang)
    return jnp.tile(jnp.concatenate([cos, cos, -sin, sin], axis=-1), (reps, 1))


def _swap_halves(w):
    half = w.shape[-1] // 2
    return jnp.concatenate([w[..., half:], w[..., :half]], axis=-1)


def _even_weights(w_in, w_q_b, w_kv_b):
    wt = w_in.T
    o_qa = 3 * W_A
    o_kv = o_qa + Q_RANK
    o_kr = o_kv + KV_RANK
    o_zb = o_kr + QK_ROPE
    w_mid = lax.optimization_barrier(wt[o_qa:o_zb]).astype(BF16)
    w_qa = w_mid[:Q_RANK]
    w_kr = w_mid[Q_RANK + KV_RANK:]
    w_kr_sw = jnp.concatenate([w_kr[QK_ROPE // 2:], w_kr[:QK_ROPE // 2]], axis=0)
    w_kv = jnp.concatenate([w_mid[Q_RANK:], w_kr_sw], axis=0)
    wq = w_q_b.reshape(Q_RANK, HEADS, QK_NOPE + QK_ROPE)
    wq_rope = wq[..., QK_NOPE:]
    w_q = jnp.concatenate([wq[..., :QK_NOPE], wq_rope, _swap_halves(wq_rope)], axis=-1)
    w_q = w_q.reshape(Q_RANK, HEADS * 256).astype(BF16)
    wkvb = w_kv_b.reshape(KV_RANK, HEADS, QK_NOPE + V_DIM)
    w_nope = jnp.transpose(wkvb[..., :QK_NOPE], (1, 2, 0)).astype(BF16)
    w_vup = jnp.transpose(wkvb[..., QK_NOPE:], (1, 0, 2)).astype(BF16)
    return wt, w_qa, w_kv, w_q, w_nope, w_vup


def _spatial_weights(w_s, b_s, seq_len):
    cl = min(CHUNK, seq_len)
    reps = CHUNK // cl
    wm = (w_s * jnp.tril(jnp.ones((CHUNK, CHUNK), w_s.dtype)))[:, :cl, :cl]
    eye = jnp.eye(reps, dtype=w_s.dtype)
    wm = jnp.einsum("ab,gts->gatbs", eye, wm).reshape(A_GROUPS, CHUNK, CHUNK).astype(BF16)
    bias = jnp.tile(b_s[:, :cl].T, (reps, 1))
    bias = jnp.repeat(bias, A_GROUP_DIM, axis=1)
    return wm, bias


def _even_layer(x, h, batch, seq_len, offset, ew, w_out, ln_v_g, ln_v_b, w_s, b_s, g_q_a, g_kv_a, g_post, g_next,
                past):
    w_in_t, w_qa, w_kv, w_q, w_nope, w_vup = ew
    m = x.shape[0]
    z_b_row0 = 3 * W_A + Q_RANK + KV_RANK + QK_ROPE
    gates = mm_gates(h, w_in_t, (0, W_A, 2 * W_A, z_b_row0))
    n_gates = 4
    qn = mm_rms(h, w_qa, g_q_a)
    cs = _rope_cs(seq_len, offset, batch)
    c, k_pe, kcat = mm_kv(h, w_kv, g_kv_a, cs)

    wm, bias = _spatial_weights(w_s, b_s, seq_len)
    blocks_per_seq = max(seq_len // CHUNK, 1)
    out_a, v_open = _mix_a_call(gates, ln_v_g, ln_v_b, wm, bias, blocks_per_seq)

    q = mm_q(qn, w_q, w_nope, cs)
    if past is None:
        out_b = mla_prefill(q, kcat.reshape(batch, seq_len, K_CAT), gates.reshape(batch, seq_len, n_gates * W_A),
                            w_vup, zb_col_block=n_gates - 1).reshape(m, W_B)
    else:
        cache_c, cache_rt, layer, page_table = past
        per = QBLOCK // seq_len
        qd = q.reshape(m // QBLOCK, HEADS, per, seq_len, K_CAT)
        qd = jnp.transpose(qd, (0, 2, 1, 3, 4)).reshape(batch, HEADS * seq_len, K_CAT)
        knew = jnp.pad(kcat.reshape(batch, seq_len, K_CAT), ((0, 0), (0, PAGE - seq_len), (0, 0)))
        o_lat = mla_decode(qd, knew, cache_c, cache_rt, layer, page_table)
        o_heads = jnp.transpose(o_lat.reshape(batch, HEADS, seq_len, KV_RANK), (1, 0, 2, 3))
        out_b = v_up(o_heads.reshape(HEADS, m, KV_RANK), w_vup, gates, zb_col0=(n_gates - 1) * W_A)
    y = mm_plain([out_a, out_b], w_out, BF16)
    x_new, h_next = residual_norm(x, y, g_post, g_next)
    return x_new, h_next, c, k_pe, v_open


def _mix_a_call(gates, ln_v_g, ln_v_b, wm, bias, blocks_per_seq):
    m = gates.shape[0]
    nblk = m // CHUNK
    row0 = lambda i: (i, 0)
    row1 = lambda i: (i, 1)
    row2 = lambda i: (i, 2)
    fixed2 = lambda i: (0, 0)
    return pl.pallas_call(
        _mix_a_kernel,
        grid=(nblk,),
        in_specs=[pl.BlockSpec((CHUNK, W_A), row0), pl.BlockSpec((CHUNK, W_A), row1),
                  pl.BlockSpec((CHUNK, W_A), row2),
                  pl.BlockSpec((1, W_A), fixed2), pl.BlockSpec((1, W_A), fixed2),
                  pl.BlockSpec((A_GROUPS, CHUNK, CHUNK), lambda i: (0, 0, 0)),
                  pl.BlockSpec((CHUNK, W_A), fixed2)],
        out_specs=[pl.BlockSpec((CHUNK, W_A), row0),
                   pl.BlockSpec((CHUNK, W_A), lambda i: (i // blocks_per_seq, 0))],
        out_shape=[jax.ShapeDtypeStruct((m, W_A), BF16),
                   jax.ShapeDtypeStruct((m // blocks_per_seq, W_A), F32)],
        compiler_params=_params_seq(("arbitrary",)),
        name="mix_a",
    )(gates, gates, gates, ln_v_g.reshape(1, W_A), ln_v_b.reshape(1, W_A), wm, bias)


def _odd_layer(x, h, batch, seq_len, conv_prev, w_in, w_dw, b_dw, ln_c_g, ln_c_b, w_out, g_post, g_next):
    m = x.shape[0]
    c = w_dw.shape[1]
    g = mm_glu(h, w_in, c).reshape(batch, seq_len, c)
    sz = mm_act(h, w_in, col0=2 * c, ncols=c, act=_silu, out_dtype=BF16).reshape(batch, seq_len, c)
    if conv_prev is None:
        y2 = conv_module(g, sz, w_dw, b_dw, ln_c_g, ln_c_b)
        new_state = g[:, seq_len - (CONV_W - 1):]
    else:
        tm = lambda a: jnp.swapaxes(a, 0, 1)
        y2, new_state = conv_step(tm(conv_prev), tm(g), tm(sz), w_dw, b_dw, ln_c_g, ln_c_b)
        y2, new_state = tm(y2), tm(new_state)
    y = mm_plain([y2.reshape(m, c)], w_out, BF16)
    x_new, h_next = residual_norm(x, y, g_post, g_next)
    return x_new, h_next, new_state


def kernel(x_prompt, x_sample, cache_kv_latent, cache_k_rope, state_conv, page_table, w_in_even, ln_v_g, ln_v_b,
           w_s, b_s, g_q_a, w_q_b, g_kv_a, w_kv_b, w_out_even, w_in_odd, w_dw, b_dw, ln_c_g, ln_c_b, w_out_odd,
           g_pre, g_post):
    bp, lp, d = x_prompt.shape
    bs, ls, _ = x_sample.shape
    past_len = page_table.shape[1] * PAGE
    depth = g_pre.shape[0]
    xp = x_prompt.reshape(bp * lp, d)
    xs = x_sample.reshape(bs * ls, d)
    lat_p, kpe_p, v_p, conv_p = [], [], [], []
    lat_s, kpe_s, v_s, conv_s = [], [], [], []
    hp = rmsnorm_cast(xp, g_pre[0])
    hs = rmsnorm_cast(xs, g_pre[0])
    for i in range(depth):
        j = i // 2
        g_next = g_pre[i + 1] if i + 1 < depth else None
        if i % 2 == 0:
            ew = _even_weights(w_in_even[j], w_q_b[j], w_kv_b[j])
            args = (ew, w_out_even[j], ln_v_g[j], ln_v_b[j], w_s[j], b_s[j], g_q_a[j], g_kv_a[j], g_post[i], g_next)
            xp, hp, c, k, v = _even_layer(xp, hp, bp, lp, 0, *args, None)
            lat_p.append(c.reshape(bp, lp, KV_RANK))
            kpe_p.append(k.reshape(bp, lp, QK_ROPE))
            v_p.append(v.reshape(bp, -1, W_A))
            cache_rt = jnp.swapaxes(cache_k_rope, 2, 3)
            xs, hs, c, k, v = _even_layer(xs, hs, bs, ls, past_len, *args,
                                          (cache_kv_latent, cache_rt, j, page_table))
            lat_s.append(c.reshape(bs, ls, KV_RANK))
            kpe_s.append(k.reshape(bs, ls, QK_ROPE))
            v_s.append(v.reshape(bs, ls, W_A))
        else:
            args = (w_in_odd[j], w_dw[j], b_dw[j], ln_c_g[j], ln_c_b[j], w_out_odd[j], g_post[i], g_next)
            xp, hp, new_state = _odd_layer(xp, hp, bp, lp, None, *args)
            conv_p.append(new_state)
            xs, hs, new_state = _odd_layer(xs, hs, bs, ls, state_conv[j], *args)
            conv_s.append(new_state)
    return (xp.reshape(bp, lp, d), xs.reshape(bs, ls, d), jnp.stack(lat_p), jnp.stack(kpe_p), jnp.stack(v_p),
            jnp.stack(conv_p), jnp.stack(lat_s), jnp.stack(kpe_s), jnp.stack(v_s), jnp.stack(conv_s))
```
